```python
import jax, jax.numpy as jnp
from jax import lax
import numpy as np

D_MODEL = 1024
BATCH = 4
SEQ = 8192
DEPTH = 2

D_MIX = 1024
DA_HEADS = 4
DA_HEAD_DIM = 64
DA_V_DIM = 2 * DA_HEAD_DIM
DA_QK = DA_HEADS * 2 * DA_HEAD_DIM
DA_WIDTH = DA_HEADS * DA_V_DIM
HG_HEADS = 4
HG_K_DIM = 128
HG_V_DIM = 128
HG_K = HG_HEADS * HG_K_DIM
HG_WIDTH = HG_HEADS * HG_V_DIM
IN_SIZES = (DA_QK, DA_QK, DA_WIDTH, HG_K, HG_K, HG_K, HG_WIDTH, HG_WIDTH)
D_IN = DA_QK * 2 + DA_WIDTH + HG_K * 3 + HG_WIDTH * 2
D_FF = 2816
CONV_WIDTH = 3
ROPE_THETA = 10000.0
Q_BLOCK = 128
CHUNK = 64
EPS = 1e-6
EXP_CLAMP = 30.0

kernel_name = "hybrid_diffattn_hgrn2_convglu_encoder"


def rms_norm(x, g):
    xf = x.astype(jnp.float32)
    y = xf * lax.rsqrt(jnp.mean(xf * xf, axis=-1, keepdims=True) + EPS)
    return (y * g.astype(jnp.float32)).astype(x.dtype)


def rope_tables(positions):
    half = DA_HEAD_DIM // 2
    inv_freq = ROPE_THETA ** (-jnp.arange(half, dtype=jnp.float32) / half)
    ang = positions.astype(jnp.float32)[:, None, None, :, None] * inv_freq
    return jnp.cos(ang), jnp.sin(ang)


def apply_rope(t, cos, sin):
    tf = t.astype(jnp.float32)
    t1, t2 = jnp.split(tf, 2, axis=-1)
    return jnp.concatenate([t1 * cos - t2 * sin, t2 * cos + t1 * sin], axis=-1).astype(t.dtype)


def split_cols(p, sizes):
    idx = []
    acc = 0
    for s in sizes[:-1]:
        acc += s
        idx.append(acc)
    return jnp.split(p, idx, axis=-1)


def to_heads(t, n_heads):
    b, s, _ = t.shape
    return t.reshape(b, s, n_heads, -1).transpose(0, 2, 1, 3)


def diff_attention(q, k, v, cos, sin, q_g, k_g, lam, lam_init, out_g):
    b, s, _ = q.shape
    q = rms_norm(q.reshape(b, s, DA_HEADS, 2, DA_HEAD_DIM), q_g)
    k = rms_norm(k.reshape(b, s, DA_HEADS, 2, DA_HEAD_DIM), k_g)
    q = apply_rope(q.transpose(0, 2, 3, 1, 4), cos, sin) * (DA_HEAD_DIM ** -0.5)
    k = apply_rope(k.transpose(0, 2, 3, 1, 4), cos, sin)
    v = to_heads(v, DA_HEADS)
    n_blocks = s // Q_BLOCK
    qb = jnp.moveaxis(q.reshape(b, DA_HEADS, 2, n_blocks, Q_BLOCK, DA_HEAD_DIM), 3, 0)

    def block(qi):
        scores = jnp.einsum('bhcqd,bhckd->bhcqk', qi, k).astype(jnp.float32)
        probs = jax.nn.softmax(scores, axis=-1)
        weights = probs[:, :, 0] - lam * probs[:, :, 1]
        return jnp.einsum('bhqk,bhkv->bhqv', weights.astype(v.dtype), v)

    o = lax.map(block, qb)
    o = o.transpose(1, 0, 3, 2, 4).reshape(b, s, DA_HEADS, DA_V_DIM)
    o = rms_norm(o, out_g) * (1.0 - lam_init)
    return o.reshape(b, s, DA_WIDTH)


def gla_chunk_scan(q, k, v, log_f):
    b, h, s, dk = q.shape
    dv = v.shape[-1]
    n = s // CHUNK

    def chunks(t):
        return jnp.moveaxis(t.reshape(b, h, n, CHUNK, t.shape[-1]), 2, 0)

    qc, kc, vc = chunks(q), chunks(k), chunks(v)
    bc = jnp.cumsum(chunks(log_f), axis=-2)
    mask = jnp.tril(jnp.ones((CHUNK, CHUNK), dtype=bool))[:, :, None]

    def step(state, inp):
        q_, k_, v_, b_ = inp
        diff = b_[:, :, :, None, :] - b_[:, :, None, :, :]
        decay = jnp.where(mask, jnp.exp(jnp.where(mask, diff, 0.0)), 0.0)
        scores = jnp.einsum('bhtd,bhsd,bhtsd->bhts', q_, k_, decay)
        b_last = b_[:, :, -1:, :]
        o = (jnp.einsum('bhts,bhsv->bhtv', scores, v_)
             + jnp.einsum('bhtd,bhdv->bhtv', q_ * jnp.exp(b_), state))
        state = (jnp.exp(b_last)[:, :, 0, :, None] * state
                 + jnp.einsum('bhsd,bhsv->bhdv', k_ * jnp.exp(b_last - b_), v_))
        return state, o

    state0 = jnp.zeros((b, h, dk, dv), jnp.float32)
    _, o = lax.scan(step, state0, (qc, kc, vc, bc))
    return jnp.moveaxis(o, 0, 2).reshape(b, h, s, dv)


def hgrn2_bidirectional(q, z_fwd, z_bwd, inp, gate, lb_fwd, lb_bwd, out_g):
    b, s, _ = q.shape
    qh = to_heads(q, HG_HEADS).astype(jnp.float32)
    ih = to_heads(inp, HG_HEADS).astype(jnp.float32)

    def gates(z, lb):
        zh = to_heads(z, HG_HEADS).astype(jnp.float32)
        lbh = lb.astype(jnp.float32).reshape(HG_HEADS, 1, HG_K_DIM)
        log_f = jax.nn.log_sigmoid(zh) + jnp.log1p(lbh * jnp.exp(jnp.minimum(-zh, EXP_CLAMP)))
        log_f = jnp.minimum(log_f, 0.0)
        key = (1.0 - lbh) * jax.nn.sigmoid(-zh)
        return key, log_f

    k_f, lf_f = gates(z_fwd, lb_fwd)
    k_b, lf_b = gates(z_bwd, lb_bwd)
    o_fwd = gla_chunk_scan(qh, k_f, ih, lf_f)
    flip = lambda t: jnp.flip(t, axis=2)
    o_bwd = flip(gla_chunk_scan(flip(qh), flip(k_b), flip(ih), flip(lf_b)))
    o = (o_fwd + o_bwd).transpose(0, 2, 1, 3).astype(q.dtype)
    g = gate.reshape(b, s, HG_HEADS, HG_V_DIM)
    o = rms_norm(o, out_g) * jax.nn.silu(g)
    return o.reshape(b, s, HG_WIDTH)


def conv_glu_ffn(h, w_up, conv_w, conv_b, w_down):
    u = h @ w_up
    up = jnp.pad(u, ((0, 0), (1, 1), (0, 0)))
    u = conv_w[0] * up[:, :-2] + conv_w[1] * up[:, 1:-1] + conv_w[2] * up[:, 2:] + conv_b
    a, v = jnp.split(u, 2, axis=-1)
    return (jax.nn.silu(a) * v) @ w_down


def setup_inputs(seed: int = 0) -> dict:
    key = jax.random.key(seed)
    ks = jax.random.split(key, 24)
    f32 = jnp.float32
    nrm = lambda k, shape, scale: jax.random.normal(k, shape, f32) * scale
    gain = lambda k, shape: 1.0 + 0.02 * jax.random.normal(k, shape, f32)
    x = jax.random.normal(ks[0], (BATCH, SEQ, D_MODEL), f32)
    offsets = jax.random.randint(ks[1], (BATCH, 1), 0, 1024, dtype=jnp.int32)
    positions = (jnp.arange(SEQ, dtype=jnp.int32)[None, :] + offsets).astype(jnp.int32)
    return {
        "x": x,
        "positions": positions,
        "mix_norm_g": gain(ks[2], (DEPTH, D_MODEL)),
        "w_in": nrm(ks[3], (DEPTH, D_MODEL, D_IN), D_MODEL ** -0.5),
        "q_norm_g": gain(ks[4], (DEPTH, DA_HEAD_DIM)),
        "k_norm_g": gain(ks[5], (DEPTH, DA_HEAD_DIM)),
        "lam_q1": nrm(ks[6], (DEPTH, DA_HEAD_DIM), 0.1),
        "lam_k1": nrm(ks[7], (DEPTH, DA_HEAD_DIM), 0.1),
        "lam_q2": nrm(ks[8], (DEPTH, DA_HEAD_DIM), 0.1),
        "lam_k2": nrm(ks[9], (DEPTH, DA_HEAD_DIM), 0.1),
        "diff_out_g": gain(ks[10], (DEPTH, DA_V_DIM)),
        "hg_lb_logits": nrm(ks[11], (DEPTH, 2, HG_K), 0.5),
        "hg_out_g": gain(ks[12], (DEPTH, HG_V_DIM)),
        "w_out": nrm(ks[13], (DEPTH, D_MIX, D_MODEL), D_MIX ** -0.5),
        "ffn_norm_g": gain(ks[14], (DEPTH, D_MODEL)),
        "w_up": nrm(ks[15], (DEPTH, D_MODEL, 2 * D_FF), D_MODEL ** -0.5),
        "conv_w": nrm(ks[16], (DEPTH, CONV_WIDTH, 2 * D_FF), CONV_WIDTH ** -0.5),
        "conv_b": nrm(ks[17], (DEPTH, 2 * D_FF), 0.02),
        "w_down": nrm(ks[18], (DEPTH, D_FF, D_MODEL), D_FF ** -0.5),
    }


def reference(x, positions, mix_norm_g, w_in, q_norm_g, k_norm_g, lam_q1, lam_k1, lam_q2, lam_k2,
              diff_out_g, hg_lb_logits, hg_out_g, w_out, ffn_norm_g, w_up, conv_w, conv_b, w_down):
    cos, sin = rope_tables(positions)
    p = jax.nn.softmax(hg_lb_logits.astype(jnp.float32), axis=0)
    lower_bounds = jnp.clip(jnp.cumsum(p, axis=0) - p[0:1], 0.0, 1.0 - 1e-4)
    for l in range(DEPTH):
        h = rms_norm(x, mix_norm_g[l])
        proj = h @ w_in[l]
        da_q, da_k, da_v, hg_q, hg_zf, hg_zb, hg_i, hg_g = split_cols(proj, IN_SIZES)
        lam_init = 0.8 - 0.6 * float(np.exp(-0.3 * l))
        lam = (jnp.exp(jnp.sum(lam_q1[l].astype(jnp.float32) * lam_k1[l].astype(jnp.float32)))
               - jnp.exp(jnp.sum(lam_q2[l].astype(jnp.float32) * lam_k2[l].astype(jnp.float32)))
               + lam_init)
        y_a = diff_attention(da_q, da_k, da_v, cos, sin, q_norm_g[l], k_norm_g[l], lam, lam_init, diff_out_g[l])
        y_b = hgrn2_bidirectional(hg_q, hg_zf, hg_zb, hg_i, hg_g,
                                  lower_bounds[l, 0], lower_bounds[l, 1], hg_out_g[l])
        x = x + jnp.concatenate([y_a, y_b.astype(y_a.dtype)], axis=-1) @ w_out[l]
        x = x + conv_glu_ffn(rms_norm(x, ffn_norm_g[l]), w_up[l], conv_w[l], conv_b[l], w_down[l])
    return x
```

```python
import functools
import math

import jax
import jax.numpy as jnp
import numpy as np
from jax import lax
from jax.experimental import pallas as pl
from jax.experimental.pallas import tpu as pltpu

F32 = jnp.float32
BF16 = jnp.bfloat16

DEPTH = 2
DA_HEADS = 4
DA_HEAD_DIM = 64
DA_V_DIM = 128
DA_QK = 512
DA_WIDTH = 512
HG_HEADS = 4
HG_DIM = 128
HG_WIDTH = 512
D_FF = 2816
ROPE_THETA = 10000.0
EPS = 1e-6
EXP_CLAMP = 30.0

LANES = 128
SUBLANES = 8
VMEM_LIMIT_BYTES = 56 * 1024 * 1024

PROJ_ROWS = 512
ATTN_Q_ROWS = 256
ATTN_K_COLS = 1024
HG_CHUNK = 64
FFN_COLS = 256
HALO = 2 * SUBLANES


def _cparams(*sem):
    return pltpu.CompilerParams(dimension_semantics=sem, vmem_limit_bytes=VMEM_LIMIT_BYTES)


def _resident(shape):
    nd = len(shape)
    return pl.BlockSpec(shape, lambda *_: (0,) * nd, pipeline_mode=pl.Buffered(1))


def _rms_rows(x, g):
    ms = jnp.mean(x * x, axis=-1, keepdims=True)
    return x * lax.rsqrt(ms + EPS) * g


def _group_mean_sq(t, gsum):
    sq = t * t
    hi = sq.astype(BF16)
    lo = (sq - hi.astype(F32)).astype(BF16)
    ss = (jnp.dot(hi, gsum, preferred_element_type=F32)
          + jnp.dot(lo, gsum, preferred_element_type=F32))
    return ss * (1.0 / DA_HEAD_DIM)


def _rope(t, cos, sin_signed, first_half):
    width = t.shape[-1]
    half = DA_HEAD_DIM // 2
    swapped = jnp.where(first_half, pltpu.roll(t, width - half, axis=1), pltpu.roll(t, half, axis=1))
    return t * cos + swapped * sin_signed


def _in_proj_kernel(x_ref, g_ref, w_ref, cos_ref, sin_ref, qg_ref, kg_ref, gsum_ref,
                    q_out, kt_out, v_out, hg_out):
    h = _rms_rows(x_ref[...], g_ref[...]).astype(BF16)

    def proj(lo, hi):
        return jnp.dot(h, w_ref[:, lo:hi], preferred_element_type=F32)

    cos = jnp.concatenate([cos_ref[...]] * (DA_QK // LANES), axis=1)
    sin = jnp.concatenate([sin_ref[...]] * (DA_QK // LANES), axis=1)
    lane = lax.broadcasted_iota(jnp.int32, (1, DA_QK), 1)
    first_half = (lane % DA_HEAD_DIM) < (DA_HEAD_DIM // 2)
    gsum = gsum_ref[...]

    q = proj(0, DA_QK)
    q = q * lax.rsqrt(_group_mean_sq(q, gsum) + EPS) * qg_ref[...]
    q = _rope(q, cos, sin, first_half) * (DA_HEAD_DIM ** -0.5)
    q_out[...] = q.astype(BF16)

    k = proj(DA_QK, 2 * DA_QK)
    k = k * lax.rsqrt(_group_mean_sq(k, gsum) + EPS) * kg_ref[...]
    k = _rope(k, cos, sin, first_half)
    kt_out[0] = k.T.astype(BF16)

    v = proj(2 * DA_QK, 2 * DA_QK + DA_WIDTH).astype(BF16)
    ones = jnp.ones((v.shape[0], DA_V_DIM), BF16)
    for hd in range(DA_HEADS):
        v_out[:, 2 * hd * DA_V_DIM:(2 * hd + 1) * DA_V_DIM] = v[:, hd * DA_V_DIM:(hd + 1) * DA_V_DIM]
        v_out[:, (2 * hd + 1) * DA_V_DIM:(2 * hd + 2) * DA_V_DIM] = ones

    hg_out[...] = proj(2 * DA_QK + DA_WIDTH, w_ref.shape[1])


def _in_proj(x2d, g, w_bf16, cos_t, sin_t, qg, kg, gsum, batch, seq):
    rows, d_model = x2d.shape
    d_in = w_bf16.shape[1]
    tm = PROJ_ROWS
    tiles_per_seq = seq // tm
    hg_cols = d_in - 2 * DA_QK - DA_WIDTH
    row_blk = lambda cols: pl.BlockSpec((tm, cols), lambda i: (i, 0))
    return pl.pallas_call(
        _in_proj_kernel,
        grid=(rows // tm,),
        in_specs=[row_blk(d_model), _resident((1, d_model)), _resident((d_model, d_in)),
                  row_blk(LANES), row_blk(LANES), _resident((1, DA_QK)), _resident((1, DA_QK)),
                  _resident((DA_QK, DA_QK))],
        out_specs=[row_blk(DA_QK),
                   pl.BlockSpec((1, DA_QK, tm), lambda i: (i // tiles_per_seq, 0, i % tiles_per_seq)),
                   row_blk(2 * DA_WIDTH), row_blk(hg_cols)],
        out_shape=[jax.ShapeDtypeStruct((rows, DA_QK), BF16),
                   jax.ShapeDtypeStruct((batch, DA_QK, seq), BF16),
                   jax.ShapeDtypeStruct((rows, 2 * DA_WIDTH), BF16),
                   jax.ShapeDtypeStruct((rows, hg_cols), F32)],
        compiler_params=_cparams("parallel"),
        name="in_proj",
    )(x2d, g, w_bf16, cos_t, sin_t, qg, kg, gsum)


def _attn_kernel(lam_ref, q_ref, kt_ref, v_ref, og_ref, o_ref, acc_ref, m_ref, *, out_scale):
    tq = q_ref.shape[1]
    seq = kt_ref.shape[2]
    tk = min(ATTN_K_COLS, seq)
    q = q_ref[0]
    lane = lax.broadcasted_iota(jnp.int32, (1, LANES), 1)
    zero = jnp.zeros_like(q)
    q_halves = (jnp.where(lane < DA_HEAD_DIM, q, zero), jnp.where(lane >= DA_HEAD_DIM, q, zero))

    acc_ref[...] = jnp.zeros(acc_ref.shape, F32)
    m_ref[...] = jnp.full(m_ref.shape, -jnp.inf, F32)

    def body(j, carry):
        start = pl.multiple_of(j * tk, tk)
        kc = kt_ref[0, :, pl.ds(start, tk)]
        vc = v_ref[0, pl.ds(start, tk), :]
        for c in range(2):
            s = jnp.dot(q_halves[c], kc, preferred_element_type=F32)
            m_prev = m_ref[c]
            m_new = jnp.maximum(m_prev, jnp.max(s, axis=1, keepdims=True))
            p = jnp.exp(s - m_new[:, :1])
            alpha = jnp.exp(m_prev - m_new)
            pv = jnp.dot(p.astype(BF16), vc, preferred_element_type=F32)
            acc_ref[c] = jnp.concatenate([alpha, alpha], axis=1) * acc_ref[c] + pv
            m_ref[c] = m_new
        return carry

    lax.fori_loop(0, seq // tk, body, 0)

    o1 = acc_ref[0, :, :DA_V_DIM] / acc_ref[0, :, DA_V_DIM:]
    o2 = acc_ref[1, :, :DA_V_DIM] / acc_ref[1, :, DA_V_DIM:]
    o = o1 - lam_ref[0, 0] * o2
    o_ref[0] = (_rms_rows(o, og_ref[...]) * out_scale).astype(BF16)


def _attention(q, kt, v, lam, out_g, out_scale):
    batch, seq, _ = q.shape
    tq = min(ATTN_Q_ROWS, seq)
    kernel = functools.partial(_attn_kernel, out_scale=out_scale)
    return pl.pallas_call(
        kernel,
        grid=(batch, DA_HEADS, seq // tq),
        in_specs=[pl.BlockSpec(memory_space=pltpu.SMEM),
                  pl.BlockSpec((1, tq, LANES), lambda b, h, i: (b, i, h)),
                  pl.BlockSpec((1, LANES, seq), lambda b, h, i: (b, h, 0)),
                  pl.BlockSpec((1, seq, 2 * DA_V_DIM), lambda b, h, i: (b, 0, h)),
                  pl.BlockSpec((1, DA_V_DIM), lambda b, h, i: (0, 0))],
        out_specs=pl.BlockSpec((1, tq, DA_V_DIM), lambda b, h, i: (b, i, h)),
        out_shape=jax.ShapeDtypeStruct((batch, seq, DA_WIDTH), BF16),
        scratch_shapes=[pltpu.VMEM((2, tq, 2 * DA_V_DIM), F32), pltpu.VMEM((2, tq, LANES), F32)],
        compiler_params=_cparams("parallel", "parallel", "arbitrary"),
        name="attention",
    )(lam, q, kt, v, out_g)


def _row(t, j):
    return jnp.broadcast_to(t[j:j + 1, :], t.shape)


def _hgrn_unit(q, z, v, lb, st_ref, reverse):
    chunk = q.shape[0]
    n_tiles = chunk // SUBLANES
    e = jnp.exp(-jnp.abs(z))
    log_sig = jnp.minimum(z, 0.0) - jnp.log1p(e)
    log_f = jnp.minimum(log_sig + jnp.log1p(lb * jnp.exp(jnp.minimum(-z, EXP_CLAMP))), 0.0)
    k = (1.0 - lb) * (jnp.where(z >= 0, e, 1.0) / (1.0 + e))

    sub = lax.broadcasted_iota(jnp.int32, (SUBLANES, LANES), 0)
    tiles = lambda a: [a[i * SUBLANES:(i + 1) * SUBLANES, :] for i in range(n_tiles)]
    q_t, k_t, lf_t = tiles(q), tiles(k), tiles(log_f)

    c_t = []
    for lf in lf_t:
        c = lf
        for sh in (1, 2, 4):
            if reverse:
                c = c + jnp.where(sub < SUBLANES - sh, pltpu.roll(c, SUBLANES - sh, axis=0), 0.0)
            else:
                c = c + jnp.where(sub >= sh, pltpu.roll(c, sh, axis=0), 0.0)
        c_t.append(c)
    edge = 0 if reverse else SUBLANES - 1
    order = (lambda i: n_tiles - 1 - i) if reverse else (lambda i: i)

    lane = lax.broadcasted_iota(jnp.int32, (SUBLANES, LANES), 1)
    a_tiles = []
    for i in range(n_tiles):
        a = jnp.zeros((SUBLANES, LANES), F32)
        for j in range(SUBLANES):
            valid = (sub <= j) if reverse else (sub >= j)
            diff = jnp.where(valid, c_t[i] - _row(c_t[i], j), 0.0)
            term = jnp.where(valid, jnp.exp(diff) * q_t[i] * _row(k_t[i], j), 0.0)
            col = jnp.sum(term, axis=1, keepdims=True)
            a = jnp.where(lane == i * SUBLANES + j, col, a)
        a_tiles.append(a)
    a_mat = jnp.concatenate(a_tiles, axis=0)[:, :chunk] if chunk < LANES else jnp.concatenate(a_tiles, axis=0)

    rows = lax.broadcasted_iota(jnp.int32, (chunk, chunk), 0)
    cols = lax.broadcasted_iota(jnp.int32, (chunk, chunk), 1)
    m = 1
    while m < n_tiles:
        blk = m * SUBLANES
        qs, ks = [], []
        for i in range(n_tiles):
            pos = order(i)
            is_query = (pos // m) % 2 == 1
            if is_query:
                qs.append(q_t[i] * jnp.exp(c_t[i]))
                ks.append(jnp.zeros((SUBLANES, LANES), F32))
            else:
                last = (pos // m) * m + m - 1
                tot = _row(c_t[order(last)], edge)
                qs.append(jnp.zeros((SUBLANES, LANES), F32))
                ks.append(k_t[i] * jnp.exp(tot - c_t[i]))
        qm = jnp.concatenate(qs, axis=0).astype(BF16)
        km = jnp.concatenate(ks, axis=0).astype(BF16)
        sc = lax.dot_general(qm, km, (((1,), (1,)), ((), ())), preferred_element_type=F32)
        if reverse:
            pair = ((rows // blk) + 1 == (cols // blk)) & ((cols // blk) % 2 == 1)
        else:
            pair = ((rows // blk) == (cols // blk) + 1) & ((rows // blk) % 2 == 1)
        a_mat = jnp.where(pair, sc, a_mat)
        new_c = []
        for i in range(n_tiles):
            pos = order(i)
            if (pos // m) % 2 == 1:
                last = (pos // m) * m - 1
                new_c.append(c_t[i] + _row(c_t[order(last)], edge))
            else:
                new_c.append(c_t[i])
        c_t = new_c
        m *= 2

    total = _row(c_t[order(n_tiles - 1)], edge)
    c_full = jnp.concatenate(c_t, axis=0)
    q_in = (q * jnp.exp(c_full)).astype(BF16)
    k_out = (k * jnp.exp(jnp.concatenate([total] * n_tiles, axis=0) - c_full)).astype(BF16)
    v16 = v.astype(BF16)
    st = st_ref[...]
    o = (jnp.dot(a_mat.astype(BF16), v16, preferred_element_type=F32)
         + lax.dot_general(q_in, st.astype(BF16), (((1,), (1,)), ((), ())), preferred_element_type=F32))
    decay = jnp.exp(total[:1, :])
    st_ref[...] = st * decay + lax.dot_general(v16, k_out, (((0,), (0,)), ((), ())),
                                               preferred_element_type=F32)
    return o


def _hgrn_kernel(qf_ref, zf_ref, vf_ref, qb_ref, zb_ref, vb_ref, lb_ref, of_ref, ob_ref, st_ref):
    @pl.when(pl.program_id(1) == 0)
    def _():
        st_ref[...] = jnp.zeros(st_ref.shape, F32)

    for hd in range(HG_HEADS):
        cs = slice(hd * HG_DIM, (hd + 1) * HG_DIM)
        of_ref[0, :, cs] = _hgrn_unit(qf_ref[0, :, cs], zf_ref[0, :, cs], vf_ref[0, :, cs],
                                      lb_ref[0:1, cs], st_ref.at[0, hd], reverse=False)
        ob_ref[0, :, cs] = _hgrn_unit(qb_ref[0, :, cs], zb_ref[0, :, cs], vb_ref[0, :, cs],
                                      lb_ref[1:2, cs], st_ref.at[1, hd], reverse=True)


def _hgrn(hg, lb):
    batch, seq, _ = hg.shape
    c = HG_CHUNK
    n = seq // c
    fwd = lambda col: pl.BlockSpec((1, c, HG_WIDTH), lambda b, j: (b, j, col))
    bwd = lambda col: pl.BlockSpec((1, c, HG_WIDTH), lambda b, j: (b, n - 1 - j, col))
    return pl.pallas_call(
        _hgrn_kernel,
        grid=(batch, n),
        in_specs=[fwd(0), fwd(1), fwd(3), bwd(0), bwd(2), bwd(3),
                  pl.BlockSpec((2, HG_WIDTH), lambda b, j: (0, 0))],
        out_specs=[fwd(0), bwd(0)],
        out_shape=[jax.ShapeDtypeStruct((batch, seq, HG_WIDTH), F32)] * 2,
        scratch_shapes=[pltpu.VMEM((2, HG_HEADS, HG_DIM, HG_DIM), F32)],
        compiler_params=_cparams("parallel", "arbitrary"),
        name="hgrn",
    )(hg, hg, hg, hg, hg, hg, lb)


def _out_proj_kernel(ya_ref, of_ref, ob_ref, gate_ref, x_ref, w_ref, hgg_ref, fg_ref, x_out, h_out):
    o = of_ref[...] + ob_ref[...]
    gate = gate_ref[...]
    parts = []
    for hd in range(HG_HEADS):
        cs = slice(hd * HG_DIM, (hd + 1) * HG_DIM)
        g = gate[:, cs]
        parts.append((_rms_rows(o[:, cs], hgg_ref[...]) * (g * jax.nn.sigmoid(g))).astype(BF16))
    y = jnp.concatenate([ya_ref[...]] + parts, axis=1)
    x1 = x_ref[...] + jnp.dot(y, w_ref[...], preferred_element_type=F32)
    x_out[...] = x1
    h_out[...] = _rms_rows(x1, fg_ref[...]).astype(BF16)


def _out_proj(ya, o_f, o_b, hg, x2d, w_bf16, hg_out_g, ffn_g):
    rows, d_model = x2d.shape
    tm = PROJ_ROWS
    gate_col = hg.shape[1] // HG_WIDTH - 1
    row_blk = lambda cols: pl.BlockSpec((tm, cols), lambda i: (i, 0))
    return pl.pallas_call(
        _out_proj_kernel,
        grid=(rows // tm,),
        in_specs=[row_blk(DA_WIDTH), row_blk(HG_WIDTH), row_blk(HG_WIDTH),
                  pl.BlockSpec((tm, HG_WIDTH), lambda i: (i, gate_col)),
                  row_blk(d_model), _resident(w_bf16.shape), _resident((1, HG_DIM)),
                  _resident((1, d_model))],
        out_specs=[row_blk(d_model), row_blk(d_model)],
        out_shape=[jax.ShapeDtypeStruct((rows, d_model), F32),
                   jax.ShapeDtypeStruct((rows, d_model), BF16)],
        compiler_params=_cparams("parallel"),
        name="out_proj",
    )(ya, o_f, o_b, hg, x2d, w_bf16, hg_out_g, ffn_g)


def _ffn_kernel(h_ref, hp_ref, hn_ref, x_ref, wup_ref, cw_ref, cb_ref, wdn_ref, o_ref, u_ref,
                *, tiles_per_seq):
    tm = h_ref.shape[0]
    i = pl.program_id(0)
    keep_prev = (i % tiles_per_seq != 0).astype(F32)
    keep_next = (i % tiles_per_seq != tiles_per_seq - 1).astype(F32)
    h_ext = jnp.concatenate([hp_ref[...], h_ref[...], hn_ref[...]], axis=0)
    row = lax.broadcasted_iota(jnp.int32, (tm + 2 * HALO, 1), 0)
    edge_scale = jnp.where(row < HALO, keep_prev, jnp.where(row >= tm + HALO, keep_next, 1.0))

    o_ref[...] = x_ref[...]
    for c in range(D_FF // FFN_COLS):
        outs = []
        for part in range(2):
            lo = part * D_FF + c * FFN_COLS
            u_ref[...] = jnp.dot(h_ext, wup_ref[:, lo:lo + FFN_COLS], preferred_element_type=F32) * edge_scale
            w = cw_ref[:, lo:lo + FFN_COLS]
            outs.append(w[0:1] * u_ref[pl.ds(HALO - 1, tm), :] + w[1:2] * u_ref[pl.ds(HALO, tm), :]
                        + w[2:3] * u_ref[pl.ds(HALO + 1, tm), :] + cb_ref[:, lo:lo + FFN_COLS])
        a, v = outs
        g = (a * jax.nn.sigmoid(a) * v).astype(BF16)
        o_ref[...] += jnp.dot(g, wdn_ref[c * FFN_COLS:(c + 1) * FFN_COLS, :], preferred_element_type=F32)


def _ffn(h2, x1, w_up, conv_w, conv_b, w_down, seq):
    rows, d_model = x1.shape
    tm = PROJ_ROWS
    tiles_per_seq = seq // tm
    per_tile = tm // HALO
    n_halo_blocks = rows // HALO
    kernel = functools.partial(_ffn_kernel, tiles_per_seq=tiles_per_seq)
    return pl.pallas_call(
        kernel,
        grid=(rows // tm,),
        in_specs=[pl.BlockSpec((tm, d_model), lambda i: (i, 0)),
                  pl.BlockSpec((HALO, d_model), lambda i: (jnp.maximum(i * per_tile - 1, 0), 0)),
                  pl.BlockSpec((HALO, d_model),
                               lambda i: (jnp.minimum((i + 1) * per_tile, n_halo_blocks - 1), 0)),
                  pl.BlockSpec((tm, d_model), lambda i: (i, 0)),
                  _resident(w_up.shape), _resident(conv_w.shape), _resident(conv_b.shape),
                  _resident(w_down.shape)],
        out_specs=pl.BlockSpec((tm, d_model), lambda i: (i, 0)),
        out_shape=jax.ShapeDtypeStruct((rows, d_model), F32),
        scratch_shapes=[pltpu.VMEM((tm + 2 * HALO, FFN_COLS), F32)],
        compiler_params=_cparams("parallel"),
        name="ffn",
    )(h2, h2, h2, x1, w_up, conv_w, conv_b, w_down)


def _rope_tables(positions):
    half = DA_HEAD_DIM // 2
    inv_freq = ROPE_THETA ** (-jnp.arange(half, dtype=F32) / half)
    ang = positions.astype(F32)[:, :, None] * inv_freq
    cos, sin = jnp.cos(ang), jnp.sin(ang)
    reps = LANES // half
    cos_t = jnp.tile(cos, (1, 1, reps))
    sin_t = jnp.tile(jnp.concatenate([-sin, sin], axis=-1), (1, 1, reps // 2))
    return cos_t.reshape(-1, LANES), sin_t.reshape(-1, LANES)


def kernel(x, positions, mix_norm_g, w_in, q_norm_g, k_norm_g, lam_q1, lam_k1, lam_q2, lam_k2,
           diff_out_g, hg_lb_logits, hg_out_g, w_out, ffn_norm_g, w_up, conv_w, conv_b, w_down):
    batch, seq, d_model = x.shape
    assert seq % PROJ_ROWS == 0 and seq % ATTN_Q_ROWS == 0 and seq % HG_CHUNK == 0
    cos_t, sin_t = _rope_tables(positions)
    p = jax.nn.softmax(hg_lb_logits.astype(F32), axis=0)
    lower_bounds = jnp.clip(jnp.cumsum(p, axis=0) - p[0:1], 0.0, 1.0 - 1e-4)
    gid = np.arange(DA_QK) // DA_HEAD_DIM
    gsum = jnp.asarray(gid[:, None] == gid[None, :], BF16)
    n_groups = DA_QK // DA_HEAD_DIM

    x2d = x.reshape(batch * seq, d_model)
    for l in range(DEPTH):
        lam_init = 0.8 - 0.6 * float(np.exp(-0.3 * l))
        lam = (jnp.exp(jnp.sum(lam_q1[l].astype(F32) * lam_k1[l].astype(F32)))
               - jnp.exp(jnp.sum(lam_q2[l].astype(F32) * lam_k2[l].astype(F32))) + lam_init)
        q, kt, v, hg = _in_proj(
            x2d, mix_norm_g[l][None, :], w_in[l].astype(BF16), cos_t, sin_t,
            jnp.tile(q_norm_g[l], n_groups)[None, :], jnp.tile(k_norm_g[l], n_groups)[None, :],
            gsum, batch, seq)
        y_a = _attention(q.reshape(batch, seq, DA_QK), kt, v.reshape(batch, seq, 2 * DA_WIDTH),
                         lam.reshape(1, 1), diff_out_g[l][None, :], 1.0 - lam_init)
        o_f, o_b = _hgrn(hg.reshape(batch, seq, -1), lower_bounds[l])
        x1, h2 = _out_proj(y_a.reshape(batch * seq, DA_WIDTH), o_f.reshape(batch * seq, HG_WIDTH),
                           o_b.reshape(batch * seq, HG_WIDTH), hg, x2d, w_out[l].astype(BF16),
                           hg_out_g[l][None, :], ffn_norm_g[l][None, :])
        x2d = _ffn(h2, x1, w_up[l].astype(BF16), conv_w[l], conv_b[l][None, :],
                   w_down[l].astype(BF16), seq)
    return x2d.reshape(batch, seq, d_model)
```

```python
import functools
import math

import jax
import jax.numpy as jnp
import numpy as np
from jax import lax
from jax.experimental import pallas as pl
from jax.experimental.pallas import tpu as pltpu

F32 = jnp.float32
BF16 = jnp.bfloat16

DEPTH = 2
DA_HEADS = 4
DA_HEAD_DIM = 64
DA_V_DIM = 128
DA_QK = 512
DA_WIDTH = 512
HG_HEADS = 4
HG_DIM = 128
HG_WIDTH = 512
D_FF = 2816
ROPE_THETA = 10000.0
EPS = 1e-6
EXP_CLAMP = 30.0

LANES = 128
SUBLANES = 8
VMEM_LIMIT_BYTES = 56 * 1024 * 1024

PROJ_ROWS = 512
ATTN_Q_COLS = 512
ATTN_K_ROWS = 512
VT_ROWS = DA_V_DIM + 2 * SUBLANES
SCORE_BOUND = 20.0
HG_CHUNK = 64
FFN_COLS = 256
HALO = 2 * SUBLANES


def _cparams(*sem):
    return pltpu.CompilerParams(dimension_semantics=sem, vmem_limit_bytes=VMEM_LIMIT_BYTES)


def _resident(shape):
    nd = len(shape)
    return pl.BlockSpec(shape, lambda *_: (0,) * nd, pipeline_mode=pl.Buffered(1))


def _rms_rows(x, g):
    ms = jnp.mean(x * x, axis=-1, keepdims=True)
    return x * lax.rsqrt(ms + EPS) * g


def _group_mean_sq(t, gsum):
    sq = t * t
    hi = sq.astype(BF16)
    lo = (sq - hi.astype(F32)).astype(BF16)
    ss = (jnp.dot(hi, gsum, preferred_element_type=F32)
          + jnp.dot(lo, gsum, preferred_element_type=F32))
    return ss * (1.0 / DA_HEAD_DIM)


def _rope(t, cos, sin_signed, first_half):
    width = t.shape[-1]
    half = DA_HEAD_DIM // 2
    swapped = jnp.where(first_half, pltpu.roll(t, width - half, axis=1), pltpu.roll(t, half, axis=1))
    return t * cos + swapped * sin_signed


def _in_proj_kernel(x_ref, g_ref, w_ref, cos_ref, sin_ref, qg_ref, kg_ref, gsum_ref,
                    qt_out, k_out, vt_out, hg_out):
    h = _rms_rows(x_ref[...], g_ref[...]).astype(BF16)

    def proj(lo, hi):
        return jnp.dot(h, w_ref[:, lo:hi], preferred_element_type=F32)

    cos = jnp.concatenate([cos_ref[...]] * (DA_QK // LANES), axis=1)
    sin = jnp.concatenate([sin_ref[...]] * (DA_QK // LANES), axis=1)
    lane = lax.broadcasted_iota(jnp.int32, (1, DA_QK), 1)
    first_half = (lane % DA_HEAD_DIM) < (DA_HEAD_DIM // 2)
    gsum = gsum_ref[...]

    q = proj(0, DA_QK)
    q = q * lax.rsqrt(_group_mean_sq(q, gsum) + EPS) * qg_ref[...]
    q = _rope(q, cos, sin, first_half) * (DA_HEAD_DIM ** -0.5)
    qt_out[0] = q.T.astype(BF16)

    k = proj(DA_QK, 2 * DA_QK)
    k = k * lax.rsqrt(_group_mean_sq(k, gsum) + EPS) * kg_ref[...]
    k_out[...] = _rope(k, cos, sin, first_half).astype(BF16)

    vt = proj(2 * DA_QK, 2 * DA_QK + DA_WIDTH).T.astype(BF16)
    ones = jnp.ones((VT_ROWS - DA_V_DIM, vt.shape[1]), BF16)
    for hd in range(DA_HEADS):
        vt_out[0, hd * VT_ROWS:hd * VT_ROWS + DA_V_DIM, :] = vt[hd * DA_V_DIM:(hd + 1) * DA_V_DIM, :]
        vt_out[0, hd * VT_ROWS + DA_V_DIM:(hd + 1) * VT_ROWS, :] = ones

    hg_out[...] = proj(2 * DA_QK + DA_WIDTH, w_ref.shape[1])


def _in_proj(x2d, g, w_bf16, cos_t, sin_t, qg, kg, gsum, batch, seq):
    rows, d_model = x2d.shape
    d_in = w_bf16.shape[1]
    tm = PROJ_ROWS
    tiles_per_seq = seq // tm
    hg_cols = d_in - 2 * DA_QK - DA_WIDTH
    row_blk = lambda cols: pl.BlockSpec((tm, cols), lambda i: (i, 0))
    t_blk = lambda ch: pl.BlockSpec((1, ch, tm), lambda i: (i // tiles_per_seq, 0, i % tiles_per_seq))
    return pl.pallas_call(
        _in_proj_kernel,
        grid=(rows // tm,),
        in_specs=[row_blk(d_model), _resident((1, d_model)), _resident((d_model, d_in)),
                  row_blk(LANES), row_blk(LANES), _resident((1, DA_QK)), _resident((1, DA_QK)),
                  _resident((DA_QK, DA_QK))],
        out_specs=[t_blk(DA_QK), row_blk(DA_QK), t_blk(DA_HEADS * VT_ROWS), row_blk(hg_cols)],
        out_shape=[jax.ShapeDtypeStruct((batch, DA_QK, seq), BF16),
                   jax.ShapeDtypeStruct((rows, DA_QK), BF16),
                   jax.ShapeDtypeStruct((batch, DA_HEADS * VT_ROWS, seq), BF16),
                   jax.ShapeDtypeStruct((rows, hg_cols), F32)],
        compiler_params=_cparams("parallel"),
        name="in_proj",
    )(x2d, g, w_bf16, cos_t, sin_t, qg, kg, gsum)


def _attn_kernel(lam_ref, qt_ref, k_ref, vt_ref, og_ref, o_ref,
                 qh_ref, kmax_ref, s_buf, p_buf, a_buf, acc_ref, m_ref, l_ref, *, out_scale):
    seq = k_ref.shape[1]
    tk = min(ATTN_K_ROWS, seq)
    n_chunks = seq // tk
    assert n_chunks % 2 == 0
    qt = qt_ref[0]
    row = lax.broadcasted_iota(jnp.int32, (LANES, 1), 0)
    zero = jnp.zeros_like(qt)
    qh_ref[0] = jnp.where(row < DA_HEAD_DIM, qt, zero)
    qh_ref[1] = jnp.where(row >= DA_HEAD_DIM, qt, zero)
    acc_ref[...] = jnp.zeros(acc_ref.shape, F32)

    def k_chunk(j):
        return k_ref[0, pl.ds(pl.multiple_of(j * tk, tk), tk), :]

    def vt_chunk(j, rows):
        return vt_ref[0, :rows, pl.ds(pl.multiple_of(j * tk, tk), tk)]

    @pl.when(pl.program_id(2) == 0)
    def _():
        grp_r = lax.broadcasted_iota(jnp.int32, (LANES, LANES), 0) // DA_HEAD_DIM
        grp_c = lax.broadcasted_iota(jnp.int32, (LANES, LANES), 1) // DA_HEAD_DIM
        gsum = jnp.where(grp_r == grp_c, 1.0, 0.0).astype(BF16)

        def kbody(j, mx):
            kf = k_chunk(j).astype(F32)
            ss = jnp.dot((kf * kf).astype(BF16), gsum, preferred_element_type=F32)
            return jnp.maximum(mx, jnp.max(ss, axis=0, keepdims=True))

        kmax_ref[...] = 1.01 * lax.fori_loop(0, n_chunks, kbody, jnp.zeros((1, LANES), F32))

    qf = qt.astype(F32)
    qsq = qf * qf
    kmax = kmax_ref[...]
    bound_sq = [jnp.max(jnp.sum(qsq[c * DA_HEAD_DIM:(c + 1) * DA_HEAD_DIM, :], axis=0, keepdims=True))
                * jnp.max(kmax[:, c * DA_HEAD_DIM:(c + 1) * DA_HEAD_DIM]) for c in range(2)]
    bounded = jnp.maximum(bound_sq[0], bound_sq[1]) <= SCORE_BOUND * SCORE_BOUND

    @pl.when(bounded)
    def _():
        def exp_scores(j, slot):
            kc = k_chunk(j)
            for c in range(2):
                s = jnp.dot(kc, qh_ref[c], preferred_element_type=F32)
                p_buf[slot, c] = jnp.exp(s).astype(BF16)

        def values(j, slot):
            vc = vt_chunk(j, VT_ROWS)
            for c in range(2):
                acc_ref[c] += jnp.dot(vc, p_buf[slot, c], preferred_element_type=F32)

        exp_scores(0, 0)

        def body(g, carry):
            j = 2 * g
            exp_scores(j + 1, 1)
            values(j, 0)
            exp_scores(j + 2, 0)
            values(j + 1, 1)
            return carry

        lax.fori_loop(0, n_chunks // 2 - 1, body, 0)
        exp_scores(n_chunks - 1, 1)
        values(n_chunks - 2, 0)
        values(n_chunks - 1, 1)

    @pl.when(jnp.logical_not(bounded))
    def _():
        l_ref[...] = jnp.zeros(l_ref.shape, F32)
        m_ref[...] = jnp.full(m_ref.shape, -jnp.inf, F32)

        def scores(j, slot):
            kc = k_chunk(j)
            for c in range(2):
                s_buf[slot, c] = jnp.dot(kc, qh_ref[c], preferred_element_type=F32)

        def softmax(slot):
            for c in range(2):
                s = s_buf[slot, c]
                m_prev = m_ref[c]
                m_new = jnp.maximum(m_prev, jnp.max(s, axis=0, keepdims=True))
                p = jnp.exp(s - m_new)
                alpha = jnp.exp(m_prev - m_new)
                l_ref[c] = alpha * l_ref[c] + jnp.sum(p, axis=0, keepdims=True)
                m_ref[c] = m_new
                p_buf[slot, c] = p.astype(BF16)
                a_buf[slot, c] = alpha

        def values(j, slot):
            vc = vt_chunk(j, DA_V_DIM)
            for c in range(2):
                acc_ref[c, :DA_V_DIM, :] = (a_buf[slot, c] * acc_ref[c, :DA_V_DIM, :]
                                            + jnp.dot(vc, p_buf[slot, c], preferred_element_type=F32))

        scores(0, 0)
        scores(1, 1)
        softmax(0)

        def body(g, carry):
            j = 2 * g
            scores(j, 0)
            softmax(1)
            values(j - 2, 0)
            scores(j + 1, 1)
            softmax(0)
            values(j - 1, 1)
            return carry

        lax.fori_loop(1, n_chunks // 2, body, 0)
        softmax(1)
        values(n_chunks - 2, 0)
        values(n_chunks - 1, 1)
        for c in range(2):
            acc_ref[c, DA_V_DIM:, :] = jnp.broadcast_to(l_ref[c], (VT_ROWS - DA_V_DIM, l_ref.shape[2]))

    o1 = acc_ref[0, :DA_V_DIM, :] / acc_ref[0, DA_V_DIM:DA_V_DIM + 1, :]
    o2 = acc_ref[1, :DA_V_DIM, :] / acc_ref[1, DA_V_DIM:DA_V_DIM + 1, :]
    ot = o1 - lam_ref[0, 0] * o2
    o_ref[0] = (_rms_rows(ot.T, og_ref[...]) * out_scale).astype(BF16)


def _attention(qt, k, vt, lam, out_g, out_scale):
    batch, seq, _ = k.shape
    tq = min(ATTN_Q_COLS, seq)
    tk = min(ATTN_K_ROWS, seq)
    kernel = functools.partial(_attn_kernel, out_scale=out_scale)
    return pl.pallas_call(
        kernel,
        grid=(batch, DA_HEADS, seq // tq),
        in_specs=[pl.BlockSpec(memory_space=pltpu.SMEM),
                  pl.BlockSpec((1, LANES, tq), lambda b, h, i: (b, h, i)),
                  pl.BlockSpec((1, seq, LANES), lambda b, h, i: (b, 0, h)),
                  pl.BlockSpec((1, VT_ROWS, seq), lambda b, h, i: (b, h, 0)),
                  pl.BlockSpec((1, DA_V_DIM), lambda b, h, i: (0, 0))],
        out_specs=pl.BlockSpec((1, tq, DA_V_DIM), lambda b, h, i: (b, i, h)),
        out_shape=jax.ShapeDtypeStruct((batch, seq, DA_WIDTH), BF16),
        scratch_shapes=[pltpu.VMEM((2, LANES, tq), BF16),
                        pltpu.VMEM((1, LANES), F32),
                        pltpu.VMEM((2, 2, tk, tq), F32),
                        pltpu.VMEM((2, 2, tk, tq), BF16),
                        pltpu.VMEM((2, 2, 1, tq), F32),
                        pltpu.VMEM((2, VT_ROWS, tq), F32),
                        pltpu.VMEM((2, 1, tq), F32),
                        pltpu.VMEM((2, 1, tq), F32)],
        compiler_params=_cparams("parallel", "parallel", "arbitrary"),
        name="attention",
    )(lam, qt, k, vt, out_g)


def _row(t, j):
    return jnp.broadcast_to(t[j:j + 1, :], t.shape)


def _hgrn_unit(q, z, v, lb, st_ref, reverse):
    chunk = q.shape[0]
    n_tiles = chunk // SUBLANES
    e = jnp.exp(-jnp.abs(z))
    log_sig = jnp.minimum(z, 0.0) - jnp.log1p(e)
    log_f = jnp.minimum(log_sig + jnp.log1p(lb * jnp.exp(jnp.minimum(-z, EXP_CLAMP))), 0.0)
    k = (1.0 - lb) * (jnp.where(z >= 0, e, 1.0) / (1.0 + e))

    sub = lax.broadcasted_iota(jnp.int32, (SUBLANES, LANES), 0)
    tiles = lambda a: [a[i * SUBLANES:(i + 1) * SUBLANES, :] for i in range(n_tiles)]
    q_t, k_t, lf_t = tiles(q), tiles(k), tiles(log_f)

    c_t = []
    for lf in lf_t:
        c = lf
        for sh in (1, 2, 4):
            if reverse:
                c = c + jnp.where(sub < SUBLANES - sh, pltpu.roll(c, SUBLANES - sh, axis=0), 0.0)
            else:
                c = c + jnp.where(sub >= sh, pltpu.roll(c, sh, axis=0), 0.0)
        c_t.append(c)
    edge = 0 if reverse else SUBLANES - 1
    order = (lambda i: n_tiles - 1 - i) if reverse else (lambda i: i)

    lane = lax.broadcasted_iota(jnp.int32, (SUBLANES, LANES), 1)
    a_tiles = []
    for i in range(n_tiles):
        a = jnp.zeros((SUBLANES, LANES), F32)
        for j in range(SUBLANES):
            valid = (sub <= j) if reverse else (sub >= j)
            diff = jnp.where(valid, c_t[i] - _row(c_t[i], j), 0.0)
            term = jnp.where(valid, jnp.exp(diff) * q_t[i] * _row(k_t[i], j), 0.0)
            col = jnp.sum(term, axis=1, keepdims=True)
            a = jnp.where(lane == i * SUBLANES + j, col, a)
        a_tiles.append(a)
    a_mat = jnp.concatenate(a_tiles, axis=0)[:, :chunk] if chunk < LANES else jnp.concatenate(a_tiles, axis=0)

    rows = lax.broadcasted_iota(jnp.int32, (chunk, chunk), 0)
    cols = lax.broadcasted_iota(jnp.int32, (chunk, chunk), 1)
    m = 1
    while m < n_tiles:
        blk = m * SUBLANES
        qs, ks = [], []
        for i in range(n_tiles):
            pos = order(i)
            is_query = (pos // m) % 2 == 1
            if is_query:
                qs.append(q_t[i] * jnp.exp(c_t[i]))
                ks.append(jnp.zeros((SUBLANES, LANES), F32))
            else:
                last = (pos // m) * m + m - 1
                tot = _row(c_t[order(last)], edge)
                qs.append(jnp.zeros((SUBLANES, LANES), F32))
                ks.append(k_t[i] * jnp.exp(tot - c_t[i]))
        qm = jnp.concatenate(qs, axis=0).astype(BF16)
        km = jnp.concatenate(ks, axis=0).astype(BF16)
        sc = lax.dot_general(qm, km, (((1,), (1,)), ((), ())), preferred_element_type=F32)
        if reverse:
            pair = ((rows // blk) + 1 == (cols // blk)) & ((cols // blk) % 2 == 1)
        else:
            pair = ((rows // blk) == (cols // blk) + 1) & ((rows // blk) % 2 == 1)
        a_mat = jnp.where(pair, sc, a_mat)
        new_c = []
        for i in range(n_tiles):
            pos = order(i)
            if (pos // m) % 2 == 1:
                last = (pos // m) * m - 1
                new_c.append(c_t[i] + _row(c_t[order(last)], edge))
            else:
                new_c.append(c_t[i])
        c_t = new_c
        m *= 2

    total = _row(c_t[order(n_tiles - 1)], edge)
    c_full = jnp.concatenate(c_t, axis=0)
    q_in = (q * jnp.exp(c_full)).astype(BF16)
    k_out = (k * jnp.exp(jnp.concatenate([total] * n_tiles, axis=0) - c_full)).astype(BF16)
    v16 = v.astype(BF16)
    st = st_ref[...]
    o = (jnp.dot(a_mat.astype(BF16), v16, preferred_element_type=F32)
         + lax.dot_general(q_in, st.astype(BF16), (((1,), (1,)), ((), ())), preferred_element_type=F32))
    decay = jnp.exp(total[:1, :])
    st_ref[...] = st * decay + lax.dot_general(v16, k_out, (((0,), (0,)), ((), ())),
                                               preferred_element_type=F32)
    return o


def _hgrn_kernel(qf_ref, zf_ref, vf_ref, qb_ref, zb_ref, vb_ref, lb_ref, of_ref, ob_ref, st_ref):
    @pl.when(pl.program_id(1) == 0)
    def _():
        st_ref[...] = jnp.zeros(st_ref.shape, F32)

    for hd in range(HG_HEADS):
        cs = slice(hd * HG_DIM, (hd + 1) * HG_DIM)
        of_ref[0, :, cs] = _hgrn_unit(qf_ref[0, :, cs], zf_ref[0, :, cs], vf_ref[0, :, cs],
                                      lb_ref[0:1, cs], st_ref.at[0, hd], reverse=False)
        ob_ref[0, :, cs] = _hgrn_unit(qb_ref[0, :, cs], zb_ref[0, :, cs], vb_ref[0, :, cs],
                                      lb_ref[1:2, cs], st_ref.at[1, hd], reverse=True)


def _hgrn(hg, lb):
    batch, seq, _ = hg.shape
    c = HG_CHUNK
    n = seq // c
    fwd = lambda col: pl.BlockSpec((1, c, HG_WIDTH), lambda b, j: (b, j, col))
    bwd = lambda col: pl.BlockSpec((1, c, HG_WIDTH), lambda b, j: (b, n - 1 - j, col))
    return pl.pallas_call(
        _hgrn_kernel,
        grid=(batch, n),
        in_specs=[fwd(0), fwd(1), fwd(3), bwd(0), bwd(2), bwd(3),
                  pl.BlockSpec((2, HG_WIDTH), lambda b, j: (0, 0))],
        out_specs=[fwd(0), bwd(0)],
        out_shape=[jax.ShapeDtypeStruct((batch, seq, HG_WIDTH), F32)] * 2,
        scratch_shapes=[pltpu.VMEM((2, HG_HEADS, HG_DIM, HG_DIM), F32)],
        compiler_params=_cparams("parallel", "arbitrary"),
        name="hgrn",
    )(hg, hg, hg, hg, hg, hg, lb)


def _out_proj_kernel(ya_ref, of_ref, ob_ref, gate_ref, x_ref, w_ref, hgg_ref, fg_ref, x_out, h_out):
    o = of_ref[...] + ob_ref[...]
    gate = gate_ref[...]
    parts = []
    for hd in range(HG_HEADS):
        cs = slice(hd * HG_DIM, (hd + 1) * HG_DIM)
        g = gate[:, cs]
        parts.append((_rms_rows(o[:, cs], hgg_ref[...]) * (g * jax.nn.sigmoid(g))).astype(BF16))
    y = jnp.concatenate([ya_ref[...]] + parts, axis=1)
    x1 = x_ref[...] + jnp.dot(y, w_ref[...], preferred_element_type=F32)
    x_out[...] = x1
    h_out[...] = _rms_rows(x1, fg_ref[...]).astype(BF16)


def _out_proj(ya, o_f, o_b, hg, x2d, w_bf16, hg_out_g, ffn_g):
    rows, d_model = x2d.shape
    tm = PROJ_ROWS
    gate_col = hg.shape[1] // HG_WIDTH - 1
    row_blk = lambda cols: pl.BlockSpec((tm, cols), lambda i: (i, 0))
    return pl.pallas_call(
        _out_proj_kernel,
        grid=(rows // tm,),
        in_specs=[row_blk(DA_WIDTH), row_blk(HG_WIDTH), row_blk(HG_WIDTH),
                  pl.BlockSpec((tm, HG_WIDTH), lambda i: (i, gate_col)),
                  row_blk(d_model), _resident(w_bf16.shape), _resident((1, HG_DIM)),
                  _resident((1, d_model))],
        out_specs=[row_blk(d_model), row_blk(d_model)],
        out_shape=[jax.ShapeDtypeStruct((rows, d_model), F32),
                   jax.ShapeDtypeStruct((rows, d_model), BF16)],
        compiler_params=_cparams("parallel"),
        name="out_proj",
    )(ya, o_f, o_b, hg, x2d, w_bf16, hg_out_g, ffn_g)


def _ffn_kernel(h_ref, hp_ref, hn_ref, x_ref, wup_ref, cw_ref, cb_ref, wdn_ref, o_ref, u_ref,
                *, tiles_per_seq):
    tm = h_ref.shape[0]
    i = pl.program_id(0)
    keep_prev = (i % tiles_per_seq != 0).astype(F32)
    keep_next = (i % tiles_per_seq != tiles_per_seq - 1).astype(F32)
    h_ext = jnp.concatenate([hp_ref[...], h_ref[...], hn_ref[...]], axis=0)
    row = lax.broadcasted_iota(jnp.int32, (tm + 2 * HALO, 1), 0)
    edge_scale = jnp.where(row < HALO, keep_prev, jnp.where(row >= tm + HALO, keep_next, 1.0))

    o_ref[...] = x_ref[...]
    for c in range(D_FF // FFN_COLS):
        outs = []
        for part in range(2):
            lo = part * D_FF + c * FFN_COLS
            u_ref[...] = jnp.dot(h_ext, wup_ref[:, lo:lo + FFN_COLS], preferred_element_type=F32) * edge_scale
            w = cw_ref[:, lo:lo + FFN_COLS]
            outs.append(w[0:1] * u_ref[pl.ds(HALO - 1, tm), :] + w[1:2] * u_ref[pl.ds(HALO, tm), :]
                        + w[2:3] * u_ref[pl.ds(HALO + 1, tm), :] + cb_ref[:, lo:lo + FFN_COLS])
        a, v = outs
        g = (a * jax.nn.sigmoid(a) * v).astype(BF16)
        o_ref[...] += jnp.dot(g, wdn_ref[c * FFN_COLS:(c + 1) * FFN_COLS, :], preferred_element_type=F32)


def _ffn(h2, x1, w_up, conv_w, conv_b, w_down, seq):
    rows, d_model = x1.shape
    tm = PROJ_ROWS
    tiles_per_seq = seq // tm
    per_tile = tm // HALO
    n_halo_blocks = rows // HALO
    kernel = functools.partial(_ffn_kernel, tiles_per_seq=tiles_per_seq)
    return pl.pallas_call(
        kernel,
        grid=(rows // tm,),
        in_specs=[pl.BlockSpec((tm, d_model), lambda i: (i, 0)),
                  pl.BlockSpec((HALO, d_model), lambda i: (jnp.maximum(i * per_tile - 1, 0), 0)),
                  pl.BlockSpec((HALO, d_model),
                               lambda i: (jnp.minimum((i + 1) * per_tile, n_halo_blocks - 1), 0)),
                  pl.BlockSpec((tm, d_model), lambda i: (i, 0)),
                  _resident(w_up.shape), _resident(conv_w.shape), _resident(conv_b.shape),
                  _resident(w_down.shape)],
        out_specs=pl.BlockSpec((tm, d_model), lambda i: (i, 0)),
        out_shape=jax.ShapeDtypeStruct((rows, d_model), F32),
        scratch_shapes=[pltpu.VMEM((tm + 2 * HALO, FFN_COLS), F32)],
        compiler_params=_cparams("parallel"),
        name="ffn",
    )(h2, h2, h2, x1, w_up, conv_w, conv_b, w_down)


def _rope_tables(positions):
    half = DA_HEAD_DIM // 2
    inv_freq = ROPE_THETA ** (-jnp.arange(half, dtype=F32) / half)
    ang = positions.astype(F32)[:, :, None] * inv_freq
    cos, sin = jnp.cos(ang), jnp.sin(ang)
    reps = LANES // half
    cos_t = jnp.tile(cos, (1, 1, reps))
    sin_t = jnp.tile(jnp.concatenate([-sin, sin], axis=-1), (1, 1, reps // 2))
    return cos_t.reshape(-1, LANES), sin_t.reshape(-1, LANES)


def kernel(x, positions, mix_norm_g, w_in, q_norm_g, k_norm_g, lam_q1, lam_k1, lam_q2, lam_k2,
           diff_out_g, hg_lb_logits, hg_out_g, w_out, ffn_norm_g, w_up, conv_w, conv_b, w_down):
    batch, seq, d_model = x.shape
    assert seq % PROJ_ROWS == 0 and seq % ATTN_Q_COLS == 0 and seq % HG_CHUNK == 0
    cos_t, sin_t = _rope_tables(positions)
    p = jax.nn.softmax(hg_lb_logits.astype(F32), axis=0)
    lower_bounds = jnp.clip(jnp.cumsum(p, axis=0) - p[0:1], 0.0, 1.0 - 1e-4)
    gid = np.arange(DA_QK) // DA_HEAD_DIM
    gsum = jnp.asarray(gid[:, None] == gid[None, :], BF16)
    n_groups = DA_QK // DA_HEAD_DIM

    x2d = x.reshape(batch * seq, d_model)
    for l in range(DEPTH):
        lam_init = 0.8 - 0.6 * float(np.exp(-0.3 * l))
        lam = (jnp.exp(jnp.sum(lam_q1[l].astype(F32) * lam_k1[l].astype(F32)))
               - jnp.exp(jnp.sum(lam_q2[l].astype(F32) * lam_k2[l].astype(F32))) + lam_init)
        qt, k, vt, hg = _in_proj(
            x2d, mix_norm_g[l][None, :], w_in[l].astype(BF16), cos_t, sin_t,
            jnp.tile(q_norm_g[l], n_groups)[None, :], jnp.tile(k_norm_g[l], n_groups)[None, :],
            gsum, batch, seq)
        y_a = _attention(qt, k.reshape(batch, seq, DA_QK), vt,
                         lam.reshape(1, 1), diff_out_g[l][None, :], 1.0 - lam_init)
        o_f, o_b = _hgrn(hg.reshape(batch, seq, -1), lower_bounds[l])
        x1, h2 = _out_proj(y_a.reshape(batch * seq, DA_WIDTH), o_f.reshape(batch * seq, HG_WIDTH),
                           o_b.reshape(batch * seq, HG_WIDTH), hg, x2d, w_out[l].astype(BF16),
                           hg_out_g[l][None, :], ffn_norm_g[l][None, :])
        x2d = _ffn(h2, x1, w_up[l].astype(BF16), conv_w[l], conv_b[l][None, :],
                   w_down[l].astype(BF16), seq)
    return x2d.reshape(batch, seq, d_model)
```

```python
import functools
import math

import jax
import jax.numpy as jnp
import numpy as np
from jax import lax
from jax.experimental import pallas as pl
from jax.experimental.pallas import tpu as pltpu

F32 = jnp.float32
BF16 = jnp.bfloat16

DEPTH = 2
DA_HEADS = 4
DA_HEAD_DIM = 64
DA_V_DIM = 128
DA_QK = 512
DA_WIDTH = 512
HG_HEADS = 4
HG_DIM = 128
HG_WIDTH = 512
D_FF = 2816
ROPE_THETA = 10000.0
EPS = 1e-6
EXP_CLAMP = 30.0

LANES = 128
SUBLANES = 8
VMEM_LIMIT_BYTES = 56 * 1024 * 1024

PROJ_ROWS = 512
ATTN_Q_COLS = 1024
ATTN_K_ROWS = 512
VT_ROWS = DA_V_DIM + 2 * SUBLANES
SCORE_BOUND = 20.0
HG_CHUNK = 64
HG_BLOCK = 512
HG_BASE_ROWS = 32
HG_BASE_BOUND = 60.0
FFN_COLS = 256
HALO = 2 * SUBLANES


def _cparams(*sem):
    return pltpu.CompilerParams(dimension_semantics=sem, vmem_limit_bytes=VMEM_LIMIT_BYTES)


def _resident(shape):
    nd = len(shape)
    return pl.BlockSpec(shape, lambda *_: (0,) * nd, pipeline_mode=pl.Buffered(1))


def _rms_rows(x, g):
    ms = jnp.mean(x * x, axis=-1, keepdims=True)
    return x * lax.rsqrt(ms + EPS) * g


def _group_mean_sq(t, gsum):
    ss = jnp.dot((t * t).astype(BF16), gsum, preferred_element_type=F32)
    return ss * (1.0 / DA_HEAD_DIM)


def _rope(t, cos, sin_signed, first_half):
    width = t.shape[-1]
    half = DA_HEAD_DIM // 2
    swapped = jnp.where(first_half, pltpu.roll(t, width - half, axis=1), pltpu.roll(t, half, axis=1))
    return t * cos + swapped * sin_signed


def _in_proj_kernel(x_ref, g_ref, w_ref, cos_ref, sin_ref, qg_ref, kg_ref, gsum_ref,
                    qt_out, k_out, vt_out, hg_out):
    h = _rms_rows(x_ref[...], g_ref[...]).astype(BF16)

    def proj(lo, hi):
        return jnp.dot(h, w_ref[:, lo:hi], preferred_element_type=F32)

    cos = jnp.concatenate([cos_ref[...]] * (DA_QK // LANES), axis=1)
    sin = jnp.concatenate([sin_ref[...]] * (DA_QK // LANES), axis=1)
    lane = lax.broadcasted_iota(jnp.int32, (1, DA_QK), 1)
    first_half = (lane % DA_HEAD_DIM) < (DA_HEAD_DIM // 2)
    gsum = gsum_ref[...]

    q = proj(0, DA_QK)
    q = q * lax.rsqrt(_group_mean_sq(q, gsum) + EPS) * qg_ref[...]
    q = _rope(q, cos, sin, first_half) * (DA_HEAD_DIM ** -0.5)
    qt_out[0] = q.T.astype(BF16)

    k = proj(DA_QK, 2 * DA_QK)
    k = k * lax.rsqrt(_group_mean_sq(k, gsum) + EPS) * kg_ref[...]
    k_out[...] = _rope(k, cos, sin, first_half).astype(BF16)

    vt = proj(2 * DA_QK, 2 * DA_QK + DA_WIDTH).T.astype(BF16)
    ones = jnp.ones((VT_ROWS - DA_V_DIM, vt.shape[1]), BF16)
    for hd in range(DA_HEADS):
        vt_out[0, hd * VT_ROWS:hd * VT_ROWS + DA_V_DIM, :] = vt[hd * DA_V_DIM:(hd + 1) * DA_V_DIM, :]
        vt_out[0, hd * VT_ROWS + DA_V_DIM:(hd + 1) * VT_ROWS, :] = ones

    hg_out[...] = proj(2 * DA_QK + DA_WIDTH, w_ref.shape[1])


def _in_proj(x2d, g, w_bf16, cos_t, sin_t, qg, kg, gsum, batch, seq):
    rows, d_model = x2d.shape
    d_in = w_bf16.shape[1]
    tm = PROJ_ROWS
    tiles_per_seq = seq // tm
    hg_cols = d_in - 2 * DA_QK - DA_WIDTH
    row_blk = lambda cols: pl.BlockSpec((tm, cols), lambda i: (i, 0))
    t_blk = lambda ch: pl.BlockSpec((1, ch, tm), lambda i: (i // tiles_per_seq, 0, i % tiles_per_seq))
    return pl.pallas_call(
        _in_proj_kernel,
        grid=(rows // tm,),
        in_specs=[row_blk(d_model), _resident((1, d_model)), _resident((d_model, d_in)),
                  row_blk(LANES), row_blk(LANES), _resident((1, DA_QK)), _resident((1, DA_QK)),
                  _resident((DA_QK, DA_QK))],
        out_specs=[t_blk(DA_QK), row_blk(DA_QK), t_blk(DA_HEADS * VT_ROWS), row_blk(hg_cols)],
        out_shape=[jax.ShapeDtypeStruct((batch, DA_QK, seq), BF16),
                   jax.ShapeDtypeStruct((rows, DA_QK), BF16),
                   jax.ShapeDtypeStruct((batch, DA_HEADS * VT_ROWS, seq), BF16),
                   jax.ShapeDtypeStruct((rows, hg_cols), F32)],
        compiler_params=_cparams("parallel"),
        name="in_proj",
    )(x2d, g, w_bf16, cos_t, sin_t, qg, kg, gsum)


def _attn_kernel(lam_ref, qt_ref, k_ref, vt_ref, og_ref, o_ref,
                 qh_ref, kmax_ref, s_buf, p_buf, a_buf, acc_ref, m_ref, l_ref, *, out_scale):
    seq = k_ref.shape[1]
    tk = min(ATTN_K_ROWS, seq)
    n_chunks = seq // tk
    assert n_chunks % 2 == 0
    qt = qt_ref[0]
    row = lax.broadcasted_iota(jnp.int32, (LANES, 1), 0)
    zero = jnp.zeros_like(qt)
    qh_ref[0] = jnp.where(row < DA_HEAD_DIM, qt, zero)
    qh_ref[1] = jnp.where(row >= DA_HEAD_DIM, qt, zero)
    acc_ref[...] = jnp.zeros(acc_ref.shape, F32)

    def k_chunk(j):
        return k_ref[0, pl.ds(pl.multiple_of(j * tk, tk), tk), :]

    def vt_chunk(j, rows):
        return vt_ref[0, :rows, pl.ds(pl.multiple_of(j * tk, tk), tk)]

    @pl.when(pl.program_id(2) == 0)
    def _():
        grp_r = lax.broadcasted_iota(jnp.int32, (LANES, LANES), 0) // DA_HEAD_DIM
        grp_c = lax.broadcasted_iota(jnp.int32, (LANES, LANES), 1) // DA_HEAD_DIM
        gsum = jnp.where(grp_r == grp_c, 1.0, 0.0).astype(BF16)

        def kbody(j, mx):
            kf = k_chunk(j).astype(F32)
            ss = jnp.dot((kf * kf).astype(BF16), gsum, preferred_element_type=F32)
            return jnp.maximum(mx, jnp.max(ss, axis=0, keepdims=True))

        kmax_ref[...] = 1.01 * lax.fori_loop(0, n_chunks, kbody, jnp.zeros((1, LANES), F32))

    qf = qt.astype(F32)
    qsq = qf * qf
    kmax = kmax_ref[...]
    bound_sq = [jnp.max(jnp.sum(qsq[c * DA_HEAD_DIM:(c + 1) * DA_HEAD_DIM, :], axis=0, keepdims=True))
                * jnp.max(kmax[:, c * DA_HEAD_DIM:(c + 1) * DA_HEAD_DIM]) for c in range(2)]
    bounded = jnp.maximum(bound_sq[0], bound_sq[1]) <= SCORE_BOUND * SCORE_BOUND

    @pl.when(bounded)
    def _():
        def exp_scores(j, slot):
            kc = k_chunk(j)
            for c in range(2):
                s = jnp.dot(kc, qh_ref[c], preferred_element_type=F32)
                p_buf[slot, c] = jnp.exp(s).astype(BF16)

        def values(j, slot):
            vc = vt_chunk(j, VT_ROWS)
            for c in range(2):
                acc_ref[c] += jnp.dot(vc, p_buf[slot, c], preferred_element_type=F32)

        exp_scores(0, 0)

        def body(g, carry):
            j = 2 * g
            exp_scores(j + 1, 1)
            values(j, 0)
            exp_scores(j + 2, 0)
            values(j + 1, 1)
            return carry

        lax.fori_loop(0, n_chunks // 2 - 1, body, 0)
        exp_scores(n_chunks - 1, 1)
        values(n_chunks - 2, 0)
        values(n_chunks - 1, 1)

    @pl.when(jnp.logical_not(bounded))
    def _():
        l_ref[...] = jnp.zeros(l_ref.shape, F32)
        m_ref[...] = jnp.full(m_ref.shape, -jnp.inf, F32)

        def scores(j, slot):
            kc = k_chunk(j)
            for c in range(2):
                s_buf[slot, c] = jnp.dot(kc, qh_ref[c], preferred_element_type=F32)

        def softmax(slot):
            for c in range(2):
                s = s_buf[slot, c]
                m_prev = m_ref[c]
                m_new = jnp.maximum(m_prev, jnp.max(s, axis=0, keepdims=True))
                p = jnp.exp(s - m_new)
                alpha = jnp.exp(m_prev - m_new)
                l_ref[c] = alpha * l_ref[c] + jnp.sum(p, axis=0, keepdims=True)
                m_ref[c] = m_new
                p_buf[slot, c] = p.astype(BF16)
                a_buf[slot, c] = alpha

        def values(j, slot):
            vc = vt_chunk(j, DA_V_DIM)
            for c in range(2):
                acc_ref[c, :DA_V_DIM, :] = (a_buf[slot, c] * acc_ref[c, :DA_V_DIM, :]
                                            + jnp.dot(vc, p_buf[slot, c], preferred_element_type=F32))

        scores(0, 0)
        scores(1, 1)
        softmax(0)

        def body(g, carry):
            j = 2 * g
            scores(j, 0)
            softmax(1)
            values(j - 2, 0)
            scores(j + 1, 1)
            softmax(0)
            values(j - 1, 1)
            return carry

        lax.fori_loop(1, n_chunks // 2, body, 0)
        softmax(1)
        values(n_chunks - 2, 0)
        values(n_chunks - 1, 1)
        for c in range(2):
            acc_ref[c, DA_V_DIM:, :] = jnp.broadcast_to(l_ref[c], (VT_ROWS - DA_V_DIM, l_ref.shape[2]))

    o1 = acc_ref[0, :DA_V_DIM, :] / acc_ref[0, DA_V_DIM:DA_V_DIM + 1, :]
    o2 = acc_ref[1, :DA_V_DIM, :] / acc_ref[1, DA_V_DIM:DA_V_DIM + 1, :]
    ot = o1 - lam_ref[0, 0] * o2
    o_ref[0] = (_rms_rows(ot.T, og_ref[...]) * out_scale).astype(BF16)


def _attention(qt, k, vt, lam, out_g, out_scale):
    batch, seq, _ = k.shape
    tq = min(ATTN_Q_COLS, seq)
    tk = min(ATTN_K_ROWS, seq)
    kernel = functools.partial(_attn_kernel, out_scale=out_scale)
    return pl.pallas_call(
        kernel,
        grid=(batch, DA_HEADS, seq // tq),
        in_specs=[pl.BlockSpec(memory_space=pltpu.SMEM),
                  pl.BlockSpec((1, LANES, tq), lambda b, h, i: (b, h, i)),
                  pl.BlockSpec((1, seq, LANES), lambda b, h, i: (b, 0, h)),
                  pl.BlockSpec((1, VT_ROWS, seq), lambda b, h, i: (b, h, 0)),
                  pl.BlockSpec((1, DA_V_DIM), lambda b, h, i: (0, 0))],
        out_specs=pl.BlockSpec((1, tq, DA_V_DIM), lambda b, h, i: (b, i, h)),
        out_shape=jax.ShapeDtypeStruct((batch, seq, DA_WIDTH), BF16),
        scratch_shapes=[pltpu.VMEM((2, LANES, tq), BF16),
                        pltpu.VMEM((1, LANES), F32),
                        pltpu.VMEM((2, 2, tk, tq), F32),
                        pltpu.VMEM((2, 2, tk, tq), BF16),
                        pltpu.VMEM((2, 2, 1, tq), F32),
                        pltpu.VMEM((2, VT_ROWS, tq), F32),
                        pltpu.VMEM((2, 1, tq), F32),
                        pltpu.VMEM((2, 1, tq), F32)],
        compiler_params=_cparams("parallel", "parallel", "arbitrary"),
        name="attention",
    )(lam, qt, k, vt, out_g)


def _row(t, j):
    return jnp.broadcast_to(t[j:j + 1, :], t.shape)


def _pair_masks(chunk, reverse):
    rows = lax.broadcasted_iota(jnp.int32, (chunk, chunk), 0)
    cols = lax.broadcasted_iota(jnp.int32, (chunk, chunk), 1)
    masks = []
    blk = SUBLANES
    while blk < chunk:
        if reverse:
            masks.append(((rows // blk) + 1 == (cols // blk)) & ((cols // blk) % 2 == 1))
        else:
            masks.append(((rows // blk) == (cols // blk) + 1) & ((rows // blk) % 2 == 1))
        blk *= 2
    return masks


def _hgrn_prep(q, z, lb, reverse):
    chunk = q.shape[0]
    n_tiles = chunk // SUBLANES
    e = jnp.exp(-jnp.abs(z))
    one_plus_e = 1.0 + e
    log_sig = jnp.minimum(z, 0.0) - jnp.log(one_plus_e)
    log_f = jnp.minimum(log_sig + jnp.log(1.0 + lb * jnp.exp(jnp.minimum(-z, EXP_CLAMP))), 0.0)
    k = (1.0 - lb) * (jnp.where(z >= 0, e, 1.0) / one_plus_e)

    sub = lax.broadcasted_iota(jnp.int32, (SUBLANES, LANES), 0)
    tiles = lambda a: [a[i * SUBLANES:(i + 1) * SUBLANES, :] for i in range(n_tiles)]
    q_t, k_t, lf_t = tiles(q), tiles(k), tiles(log_f)
    c_t = []
    for lf in lf_t:
        c = lf
        for sh in (1, 2, 4):
            if reverse:
                c = c + jnp.where(sub < SUBLANES - sh, pltpu.roll(c, SUBLANES - sh, axis=0), 0.0)
            else:
                c = c + jnp.where(sub >= sh, pltpu.roll(c, sh, axis=0), 0.0)
        c_t.append(c)
    return k, q_t, k_t, c_t


def _hgrn_finish(q, v, k, q_t, k_t, c_t, st_ref, base_mask, pair_masks, reverse, bounded):
    chunk = q.shape[0]
    n_tiles = chunk // SUBLANES
    sub = lax.broadcasted_iota(jnp.int32, (SUBLANES, LANES), 0)
    edge = 0 if reverse else SUBLANES - 1
    order = (lambda i: n_tiles - 1 - i) if reverse else (lambda i: i)
    nt_dims = (((1,), (1,)), ((), ()))

    def extend(c_t, m):
        out = []
        for i in range(n_tiles):
            pos = order(i)
            if (pos // m) % 2 == 1:
                last = (pos // m) * m - 1
                out.append(c_t[i] + _row(c_t[order(last)], edge))
            else:
                out.append(c_t[i])
        return out

    m = 1
    level = 0
    if bounded:
        while m * SUBLANES < min(HG_BASE_ROWS, chunk):
            c_t = extend(c_t, m)
            m *= 2
            level += 1
        qm = jnp.concatenate([q_t[i] * jnp.exp(c_t[i]) for i in range(n_tiles)], axis=0).astype(BF16)
        km = jnp.concatenate([k_t[i] * jnp.exp(-c_t[i]) for i in range(n_tiles)], axis=0).astype(BF16)
        sc = lax.dot_general(qm, km, nt_dims, preferred_element_type=F32)
        a_mat = jnp.where(base_mask, sc, 0.0)
    else:
        lane = lax.broadcasted_iota(jnp.int32, (SUBLANES, LANES), 1)
        a_tiles = []
        for i in range(n_tiles):
            a = jnp.zeros((SUBLANES, LANES), F32)
            for j in range(SUBLANES):
                valid = (sub <= j) if reverse else (sub >= j)
                term = jnp.where(valid, jnp.exp(c_t[i] - _row(c_t[i], j)) * q_t[i] * _row(k_t[i], j), 0.0)
                col = jnp.sum(term, axis=1, keepdims=True)
                a = jnp.where(lane == i * SUBLANES + j, col, a)
            a_tiles.append(a)
        a_mat = jnp.concatenate(a_tiles, axis=0)
        if chunk < LANES:
            a_mat = a_mat[:, :chunk]

    while m < n_tiles:
        qs, ks = [], []
        for i in range(n_tiles):
            pos = order(i)
            is_query = (pos // m) % 2 == 1
            if is_query:
                qs.append(q_t[i] * jnp.exp(c_t[i]))
                ks.append(jnp.zeros((SUBLANES, LANES), F32))
            else:
                last = (pos // m) * m + m - 1
                tot = _row(c_t[order(last)], edge)
                qs.append(jnp.zeros((SUBLANES, LANES), F32))
                ks.append(k_t[i] * jnp.exp(tot - c_t[i]))
        qm = jnp.concatenate(qs, axis=0).astype(BF16)
        km = jnp.concatenate(ks, axis=0).astype(BF16)
        sc = lax.dot_general(qm, km, nt_dims, preferred_element_type=F32)
        a_mat = jnp.where(pair_masks[level], sc, a_mat)
        c_t = extend(c_t, m)
        m *= 2
        level += 1

    total = _row(c_t[order(n_tiles - 1)], edge)
    c_full = jnp.concatenate(c_t, axis=0)
    q_in = (q * jnp.exp(c_full)).astype(BF16)
    k_out = (k * jnp.exp(jnp.concatenate([total] * n_tiles, axis=0) - c_full)).astype(BF16)
    v16 = v.astype(BF16)
    st = st_ref[...]
    o = (jnp.dot(a_mat.astype(BF16), v16, preferred_element_type=F32)
         + lax.dot_general(q_in, st.astype(BF16), nt_dims, preferred_element_type=F32))
    decay = jnp.exp(total[:1, :])
    st_ref[...] = st * decay + lax.dot_general(v16, k_out, (((0,), (0,)), ((), ())),
                                               preferred_element_type=F32)
    return o


def _hgrn_kernel(qf_ref, zf_ref, vf_ref, qb_ref, zb_ref, vb_ref, lb_ref, of_ref, ob_ref, st_ref):
    @pl.when(pl.program_id(1) == 0)
    def _():
        st_ref[...] = jnp.zeros(st_ref.shape, F32)

    c = HG_CHUNK
    base = min(HG_BASE_ROWS, c)
    n_chunks = qf_ref.shape[1] // c
    rows = lax.broadcasted_iota(jnp.int32, (c, c), 0)
    cols = lax.broadcasted_iota(jnp.int32, (c, c), 1)
    same_base = (rows // base) == (cols // base)
    dirs = ((qf_ref, zf_ref, vf_ref, of_ref, same_base & (cols <= rows), _pair_masks(c, reverse=False)),
            (qb_ref, zb_ref, vb_ref, ob_ref, same_base & (cols >= rows), _pair_masks(c, reverse=True)))

    def body(j, carry):
        row_sel = (pl.ds(pl.multiple_of(j * c, c), c),
                   pl.ds(pl.multiple_of((n_chunks - 1 - j) * c, c), c))
        low = jnp.zeros((1, HG_WIDTH), F32)
        for d in range(2):
            zneg = jnp.minimum(dirs[d][1][0, row_sel[d], :], 0.0)
            for blk in range(c // base):
                low = jnp.minimum(low, jnp.sum(zneg[blk * base:(blk + 1) * base, :], axis=0, keepdims=True))
        bounded = jnp.min(low) - base * math.log(2.0) >= -HG_BASE_BOUND

        def chunk_step(is_bounded):
            for hd in range(HG_HEADS):
                cs = slice(hd * HG_DIM, (hd + 1) * HG_DIM)
                for d, (q_ref, z_ref, v_ref, o_ref, base_mask, pair_masks) in enumerate(dirs):
                    q = q_ref[0, row_sel[d], cs]
                    k, q_t, k_t, c_t = _hgrn_prep(q, z_ref[0, row_sel[d], cs], lb_ref[d:d + 1, cs],
                                                  reverse=bool(d))
                    o_ref[0, row_sel[d], cs] = _hgrn_finish(
                        q, v_ref[0, row_sel[d], cs], k, q_t, k_t, c_t, st_ref.at[d, hd],
                        base_mask, pair_masks, reverse=bool(d), bounded=is_bounded)

        pl.when(bounded)(functools.partial(chunk_step, True))
        pl.when(jnp.logical_not(bounded))(functools.partial(chunk_step, False))
        return carry

    lax.fori_loop(0, n_chunks, body, 0)


def _hgrn(hg, lb):
    batch, seq, _ = hg.shape
    c = min(HG_BLOCK, seq)
    n = seq // c
    fwd = lambda col: pl.BlockSpec((1, c, HG_WIDTH), lambda b, j: (b, j, col))
    bwd = lambda col: pl.BlockSpec((1, c, HG_WIDTH), lambda b, j: (b, n - 1 - j, col))
    return pl.pallas_call(
        _hgrn_kernel,
        grid=(batch, n),
        in_specs=[fwd(0), fwd(1), fwd(3), bwd(0), bwd(2), bwd(3),
                  pl.BlockSpec((2, HG_WIDTH), lambda b, j: (0, 0))],
        out_specs=[fwd(0), bwd(0)],
        out_shape=[jax.ShapeDtypeStruct((batch, seq, HG_WIDTH), F32)] * 2,
        scratch_shapes=[pltpu.VMEM((2, HG_HEADS, HG_DIM, HG_DIM), F32)],
        compiler_params=_cparams("parallel", "arbitrary"),
        name="hgrn",
    )(hg, hg, hg, hg, hg, hg, lb)


def _out_proj_kernel(ya_ref, of_ref, ob_ref, gate_ref, x_ref, w_ref, hgg_ref, fg_ref, x_out, h_out):
    o = of_ref[...] + ob_ref[...]
    gate = gate_ref[...]
    parts = []
    for hd in range(HG_HEADS):
        cs = slice(hd * HG_DIM, (hd + 1) * HG_DIM)
        g = gate[:, cs]
        parts.append((_rms_rows(o[:, cs], hgg_ref[...]) * (g * jax.nn.sigmoid(g))).astype(BF16))
    y = jnp.concatenate([ya_ref[...]] + parts, axis=1)
    x1 = x_ref[...] + jnp.dot(y, w_ref[...], preferred_element_type=F32)
    x_out[...] = x1
    h_out[...] = _rms_rows(x1, fg_ref[...]).astype(BF16)


def _out_proj(ya, o_f, o_b, hg, x2d, w_bf16, hg_out_g, ffn_g):
    rows, d_model = x2d.shape
    tm = PROJ_ROWS
    gate_col = hg.shape[1] // HG_WIDTH - 1
    row_blk = lambda cols: pl.BlockSpec((tm, cols), lambda i: (i, 0))
    return pl.pallas_call(
        _out_proj_kernel,
        grid=(rows // tm,),
        in_specs=[row_blk(DA_WIDTH), row_blk(HG_WIDTH), row_blk(HG_WIDTH),
                  pl.BlockSpec((tm, HG_WIDTH), lambda i: (i, gate_col)),
                  row_blk(d_model), _resident(w_bf16.shape), _resident((1, HG_DIM)),
                  _resident((1, d_model))],
        out_specs=[row_blk(d_model), row_blk(d_model)],
        out_shape=[jax.ShapeDtypeStruct((rows, d_model), F32),
                   jax.ShapeDtypeStruct((rows, d_model), BF16)],
        compiler_params=_cparams("parallel"),
        name="out_proj",
    )(ya, o_f, o_b, hg, x2d, w_bf16, hg_out_g, ffn_g)


def _ffn_kernel(h_ref, hp_ref, hn_ref, x_ref, wup_ref, cw_ref, cb_ref, wdn_ref, o_ref,
                hext_ref, u_buf, g_ref, *, tiles_per_seq):
    tm = h_ref.shape[0]
    i = pl.program_id(0)
    first = i % tiles_per_seq == 0
    last = i % tiles_per_seq == tiles_per_seq - 1
    hext_ref[:HALO] = jnp.where(first, jnp.zeros_like(hp_ref[...]), hp_ref[...])
    hext_ref[HALO:HALO + tm] = h_ref[...]
    hext_ref[HALO + tm:] = jnp.where(last, jnp.zeros_like(hn_ref[...]), hn_ref[...])
    n_chunks = D_FF // FFN_COLS

    def up(c, slot):
        for part in range(2):
            lo = part * D_FF + c * FFN_COLS
            u_buf[slot, part] = jnp.dot(hext_ref[...], wup_ref[:, lo:lo + FFN_COLS],
                                        preferred_element_type=F32)

    def conv_gate(c, slot):
        outs = []
        for part in range(2):
            lo = part * D_FF + c * FFN_COLS
            w = cw_ref[:, lo:lo + FFN_COLS]
            outs.append(w[0:1] * u_buf[slot, part, pl.ds(HALO - 1, tm), :]
                        + w[1:2] * u_buf[slot, part, pl.ds(HALO, tm), :]
                        + w[2:3] * u_buf[slot, part, pl.ds(HALO + 1, tm), :]
                        + cb_ref[:, lo:lo + FFN_COLS])
        a, v = outs
        g_ref[:, c * FFN_COLS:(c + 1) * FFN_COLS] = (a * jax.nn.sigmoid(a) * v).astype(BF16)

    up(0, 0)
    for c in range(n_chunks):
        if c + 1 < n_chunks:
            up(c + 1, (c + 1) % 2)
        conv_gate(c, c % 2)
    o_ref[...] = x_ref[...] + jnp.dot(g_ref[...], wdn_ref[...], preferred_element_type=F32)


def _ffn(h2, x1, w_up, conv_w, conv_b, w_down, seq):
    rows, d_model = x1.shape
    tm = PROJ_ROWS
    tiles_per_seq = seq // tm
    per_tile = tm // HALO
    n_halo_blocks = rows // HALO
    kernel = functools.partial(_ffn_kernel, tiles_per_seq=tiles_per_seq)
    return pl.pallas_call(
        kernel,
        grid=(rows // tm,),
        in_specs=[pl.BlockSpec((tm, d_model), lambda i: (i, 0)),
                  pl.BlockSpec((HALO, d_model), lambda i: (jnp.maximum(i * per_tile - 1, 0), 0)),
                  pl.BlockSpec((HALO, d_model),
                               lambda i: (jnp.minimum((i + 1) * per_tile, n_halo_blocks - 1), 0)),
                  pl.BlockSpec((tm, d_model), lambda i: (i, 0)),
                  _resident(w_up.shape), _resident(conv_w.shape), _resident(conv_b.shape),
                  _resident(w_down.shape)],
        out_specs=pl.BlockSpec((tm, d_model), lambda i: (i, 0)),
        out_shape=jax.ShapeDtypeStruct((rows, d_model), F32),
        scratch_shapes=[pltpu.VMEM((tm + 2 * HALO, d_model), BF16),
                        pltpu.VMEM((2, 2, tm + 2 * HALO, FFN_COLS), F32),
                        pltpu.VMEM((tm, D_FF), BF16)],
        compiler_params=_cparams("parallel"),
        name="ffn",
    )(h2, h2, h2, x1, w_up, conv_w, conv_b, w_down)


def _rope_tables(positions):
    half = DA_HEAD_DIM // 2
    inv_freq = ROPE_THETA ** (-jnp.arange(half, dtype=F32) / half)
    ang = positions.astype(F32)[:, :, None] * inv_freq
    cos, sin = jnp.cos(ang), jnp.sin(ang)
    reps = LANES // half
    cos_t = jnp.tile(cos, (1, 1, reps))
    sin_t = jnp.tile(jnp.concatenate([-sin, sin], axis=-1), (1, 1, reps // 2))
    return cos_t.reshape(-1, LANES), sin_t.reshape(-1, LANES)


def kernel(x, positions, mix_norm_g, w_in, q_norm_g, k_norm_g, lam_q1, lam_k1, lam_q2, lam_k2,
           diff_out_g, hg_lb_logits, hg_out_g, w_out, ffn_norm_g, w_up, conv_w, conv_b, w_down):
    batch, seq, d_model = x.shape
    assert seq % PROJ_ROWS == 0 and seq % ATTN_Q_COLS == 0 and seq % HG_BLOCK == 0
    cos_t, sin_t = _rope_tables(positions)
    p = jax.nn.softmax(hg_lb_logits.astype(F32), axis=0)
    lower_bounds = jnp.clip(jnp.cumsum(p, axis=0) - p[0:1], 0.0, 1.0 - 1e-4)
    gid = np.arange(DA_QK) // DA_HEAD_DIM
    gsum = jnp.asarray(gid[:, None] == gid[None, :], BF16)
    n_groups = DA_QK // DA_HEAD_DIM

    x2d = x.reshape(batch * seq, d_model)
    for l in range(DEPTH):
        lam_init = 0.8 - 0.6 * float(np.exp(-0.3 * l))
        lam = (jnp.exp(jnp.sum(lam_q1[l].astype(F32) * lam_k1[l].astype(F32)))
               - jnp.exp(jnp.sum(lam_q2[l].astype(F32) * lam_k2[l].astype(F32))) + lam_init)
        qt, k, vt, hg = _in_proj(
            x2d, mix_norm_g[l][None, :], w_in[l].astype(BF16), cos_t, sin_t,
            jnp.tile(q_norm_g[l], n_groups)[None, :], jnp.tile(k_norm_g[l], n_groups)[None, :],
            gsum, batch, seq)
        y_a = _attention(qt, k.reshape(batch, seq, DA_QK), vt,
                         lam.reshape(1, 1), diff_out_g[l][None, :], 1.0 - lam_init)
        o_f, o_b = _hgrn(hg.reshape(batch, seq, -1), lower_bounds[l])
        x1, h2 = _out_proj(y_a.reshape(batch * seq, DA_WIDTH), o_f.reshape(batch * seq, HG_WIDTH),
                           o_b.reshape(batch * seq, HG_WIDTH), hg, x2d, w_out[l].astype(BF16),
                           hg_out_g[l][None, :], ffn_norm_g[l][None, :])
        x2d = _ffn(h2, x1, w_up[l].astype(BF16), conv_w[l], conv_b[l][None, :],
                   w_down[l].astype(BF16), seq)
    return x2d.reshape(batch, seq, d_model)
```

```python
import functools
import math

import jax
import jax.numpy as jnp
import numpy as np
from jax import lax
from jax.experimental import pallas as pl
from jax.experimental.pallas import tpu as pltpu

F32 = jnp.float32
BF16 = jnp.bfloat16

DEPTH = 2
DA_HEADS = 4
DA_HEAD_DIM = 64
DA_V_DIM = 128
DA_QK = 512
DA_WIDTH = 512
HG_HEADS = 4
HG_DIM = 128
HG_WIDTH = 512
D_FF = 2816
ROPE_THETA = 10000.0
EPS = 1e-6
EXP_CLAMP = 30.0

LANES = 128
SUBLANES = 8
VMEM_LIMIT_BYTES = 56 * 1024 * 1024

PROJ_ROWS = 512
ATTN_Q_COLS = 2048
ATTN_K_ROWS = 512
ATTN_K_ROWS_GENERAL = 256
SCORE_BOUND = 20.0
HG_CHUNK = 64
HG_BLOCK = 512
HG_BASE_ROWS = 32
HG_BASE_BOUND = 60.0
FFN_COLS = 256
FFN_DOWN_CHUNKS = 11
HALO = 2 * SUBLANES


def _cparams(*sem):
    return pltpu.CompilerParams(dimension_semantics=sem, vmem_limit_bytes=VMEM_LIMIT_BYTES)


def _resident(shape):
    nd = len(shape)
    return pl.BlockSpec(shape, lambda *_: (0,) * nd, pipeline_mode=pl.Buffered(1))


def _rms_rows(x, g):
    ms = jnp.mean(x * x, axis=-1, keepdims=True)
    return x * lax.rsqrt(ms + EPS) * g


def _group_mean_sq(t, gsum):
    ss = jnp.dot((t * t).astype(BF16), gsum, preferred_element_type=F32)
    return ss * (1.0 / DA_HEAD_DIM)


def _rope(t, cos, sin_signed, first_half):
    width = t.shape[-1]
    half = DA_HEAD_DIM // 2
    swapped = jnp.where(first_half, pltpu.roll(t, width - half, axis=1), pltpu.roll(t, half, axis=1))
    return t * cos + swapped * sin_signed


def _in_proj_kernel(x_ref, g_ref, w_ref, cos_ref, sin_ref, qg_ref, kg_ref, gsum_ref,
                    qt_out, k_out, vt_out, hg_out):
    h = _rms_rows(x_ref[...], g_ref[...]).astype(BF16)

    def proj(lo, hi):
        return jnp.dot(h, w_ref[:, lo:hi], preferred_element_type=F32)

    cos = jnp.concatenate([cos_ref[...]] * (DA_QK // LANES), axis=1)
    sin = jnp.concatenate([sin_ref[...]] * (DA_QK // LANES), axis=1)
    lane = lax.broadcasted_iota(jnp.int32, (1, DA_QK), 1)
    first_half = (lane % DA_HEAD_DIM) < (DA_HEAD_DIM // 2)
    gsum = gsum_ref[...]

    q = proj(0, DA_QK)
    q = q * lax.rsqrt(_group_mean_sq(q, gsum) + EPS) * qg_ref[...]
    q = _rope(q, cos, sin, first_half) * (DA_HEAD_DIM ** -0.5)
    qt_out[0] = q.T.astype(BF16)

    k = proj(DA_QK, 2 * DA_QK)
    k = k * lax.rsqrt(_group_mean_sq(k, gsum) + EPS) * kg_ref[...]
    k_out[...] = _rope(k, cos, sin, first_half).astype(BF16)

    vt_out[0] = proj(2 * DA_QK, 2 * DA_QK + DA_WIDTH).T.astype(BF16)

    hg_out[...] = proj(2 * DA_QK + DA_WIDTH, w_ref.shape[1])


def _in_proj(x2d, g, w_bf16, cos_t, sin_t, qg, kg, gsum, batch, seq):
    rows, d_model = x2d.shape
    d_in = w_bf16.shape[1]
    tm = PROJ_ROWS
    tiles_per_seq = seq // tm
    hg_cols = d_in - 2 * DA_QK - DA_WIDTH
    row_blk = lambda cols: pl.BlockSpec((tm, cols), lambda i: (i, 0))
    t_blk = lambda ch: pl.BlockSpec((1, ch, tm), lambda i: (i // tiles_per_seq, 0, i % tiles_per_seq))
    return pl.pallas_call(
        _in_proj_kernel,
        grid=(rows // tm,),
        in_specs=[row_blk(d_model), _resident((1, d_model)), _resident((d_model, d_in)),
                  row_blk(LANES), row_blk(LANES), _resident((1, DA_QK)), _resident((1, DA_QK)),
                  _resident((DA_QK, DA_QK))],
        out_specs=[t_blk(DA_QK), row_blk(DA_QK), t_blk(DA_WIDTH), row_blk(hg_cols)],
        out_shape=[jax.ShapeDtypeStruct((batch, DA_QK, seq), BF16),
                   jax.ShapeDtypeStruct((rows, DA_QK), BF16),
                   jax.ShapeDtypeStruct((batch, DA_WIDTH, seq), BF16),
                   jax.ShapeDtypeStruct((rows, hg_cols), F32)],
        compiler_params=_cparams("parallel"),
        name="in_proj",
    )(x2d, g, w_bf16, cos_t, sin_t, qg, kg, gsum)


def _attn_kernel(lam_ref, qt_ref, k_ref, vt_ref, og_ref, o_ref,
                 qh_ref, kmax_ref, s_buf, p_buf, a_buf, acc_ref, m_ref, l_ref, *, out_scale):
    seq = k_ref.shape[1]
    tk = p_buf.shape[2]
    tk_gen = s_buf.shape[2]
    assert (seq // tk) % 2 == 0 and (seq // tk_gen) % 2 == 0
    qt = qt_ref[0]
    row = lax.broadcasted_iota(jnp.int32, (LANES, 1), 0)
    zero = jnp.zeros_like(qt)
    qh_ref[0] = jnp.where(row < DA_HEAD_DIM, qt, zero)
    qh_ref[1] = jnp.where(row >= DA_HEAD_DIM, qt, zero)
    acc_ref[...] = jnp.zeros(acc_ref.shape, F32)
    l_ref[...] = jnp.zeros(l_ref.shape, F32)

    def k_chunk(j, size):
        return k_ref[0, pl.ds(pl.multiple_of(j * size, size), size), :]

    def vt_chunk(j, size):
        return vt_ref[0, :, pl.ds(pl.multiple_of(j * size, size), size)]

    @pl.when(pl.program_id(2) == 0)
    def _():
        grp_r = lax.broadcasted_iota(jnp.int32, (LANES, LANES), 0) // DA_HEAD_DIM
        grp_c = lax.broadcasted_iota(jnp.int32, (LANES, LANES), 1) // DA_HEAD_DIM
        gsum = jnp.where(grp_r == grp_c, 1.0, 0.0).astype(BF16)

        def kbody(j, mx):
            kf = k_chunk(j, tk).astype(F32)
            ss = jnp.dot((kf * kf).astype(BF16), gsum, preferred_element_type=F32)
            return jnp.maximum(mx, jnp.max(ss, axis=0, keepdims=True))

        kmax_ref[...] = 1.01 * lax.fori_loop(0, seq // tk, kbody, jnp.zeros((1, LANES), F32))

    qf = qt.astype(F32)
    qsq = qf * qf
    kmax = kmax_ref[...]
    bound_sq = [jnp.max(jnp.sum(qsq[c * DA_HEAD_DIM:(c + 1) * DA_HEAD_DIM, :], axis=0, keepdims=True))
                * jnp.max(kmax[:, c * DA_HEAD_DIM:(c + 1) * DA_HEAD_DIM]) for c in range(2)]
    bounded = jnp.maximum(bound_sq[0], bound_sq[1]) <= SCORE_BOUND * SCORE_BOUND

    @pl.when(bounded)
    def _():
        n_chunks = seq // tk

        def exp_scores(j, slot):
            kc = k_chunk(j, tk)
            for c in range(2):
                s = jnp.dot(kc, qh_ref[c], preferred_element_type=F32)
                p = jnp.exp(s)
                l_ref[c] += jnp.sum(p, axis=0, keepdims=True)
                p_buf[slot, c] = p.astype(BF16)

        def values(j, slot):
            vc = vt_chunk(j, tk)
            for c in range(2):
                acc_ref[c] += jnp.dot(vc, p_buf[slot, c], preferred_element_type=F32)

        exp_scores(0, 0)

        def body(g, carry):
            j = 2 * g
            exp_scores(j + 1, 1)
            values(j, 0)
            exp_scores(j + 2, 0)
            values(j + 1, 1)
            return carry

        lax.fori_loop(0, n_chunks // 2 - 1, body, 0)
        exp_scores(n_chunks - 1, 1)
        values(n_chunks - 2, 0)
        values(n_chunks - 1, 1)

    @pl.when(jnp.logical_not(bounded))
    def _():
        n_chunks = seq // tk_gen
        m_ref[...] = jnp.full(m_ref.shape, -jnp.inf, F32)

        def scores(j, slot):
            kc = k_chunk(j, tk_gen)
            for c in range(2):
                s_buf[slot, c] = jnp.dot(kc, qh_ref[c], preferred_element_type=F32)

        def softmax(slot):
            for c in range(2):
                s = s_buf[slot, c]
                m_prev = m_ref[c]
                m_new = jnp.maximum(m_prev, jnp.max(s, axis=0, keepdims=True))
                p = jnp.exp(s - m_new)
                alpha = jnp.exp(m_prev - m_new)
                l_ref[c] = alpha * l_ref[c] + jnp.sum(p, axis=0, keepdims=True)
                m_ref[c] = m_new
                p_buf[slot, c, :tk_gen, :] = p.astype(BF16)
                a_buf[slot, c] = alpha

        def values(j, slot):
            vc = vt_chunk(j, tk_gen)
            for c in range(2):
                acc_ref[c] = (a_buf[slot, c] * acc_ref[c]
                              + jnp.dot(vc, p_buf[slot, c, :tk_gen, :], preferred_element_type=F32))

        scores(0, 0)
        scores(1, 1)
        softmax(0)

        def body(g, carry):
            j = 2 * g
            scores(j, 0)
            softmax(1)
            values(j - 2, 0)
            scores(j + 1, 1)
            softmax(0)
            values(j - 1, 1)
            return carry

        lax.fori_loop(1, n_chunks // 2, body, 0)
        softmax(1)
        values(n_chunks - 2, 0)
        values(n_chunks - 1, 1)

    ot = acc_ref[0] / l_ref[0] - lam_ref[0, 0] * (acc_ref[1] / l_ref[1])
    o_ref[0] = (_rms_rows(ot.T, og_ref[...]) * out_scale).astype(BF16)


def _attention(qt, k, vt, lam, out_g, out_scale):
    batch, seq, _ = k.shape
    tq = min(ATTN_Q_COLS, seq)
    tk = min(ATTN_K_ROWS, seq // 2)
    tk_gen = min(ATTN_K_ROWS_GENERAL, seq // 2)
    kernel = functools.partial(_attn_kernel, out_scale=out_scale)
    return pl.pallas_call(
        kernel,
        grid=(batch, DA_HEADS, seq // tq),
        in_specs=[pl.BlockSpec(memory_space=pltpu.SMEM),
                  pl.BlockSpec((1, LANES, tq), lambda b, h, i: (b, h, i)),
                  pl.BlockSpec((1, seq, LANES), lambda b, h, i: (b, 0, h)),
                  pl.BlockSpec((1, DA_V_DIM, seq), lambda b, h, i: (b, h, 0)),
                  pl.BlockSpec((1, DA_V_DIM), lambda b, h, i: (0, 0))],
        out_specs=pl.BlockSpec((1, tq, DA_V_DIM), lambda b, h, i: (b, i, h)),
        out_shape=jax.ShapeDtypeStruct((batch, seq, DA_WIDTH), BF16),
        scratch_shapes=[pltpu.VMEM((2, LANES, tq), BF16),
                        pltpu.VMEM((1, LANES), F32),
                        pltpu.VMEM((2, 2, tk_gen, tq), F32),
                        pltpu.VMEM((2, 2, tk, tq), BF16),
                        pltpu.VMEM((2, 2, 1, tq), F32),
                        pltpu.VMEM((2, DA_V_DIM, tq), F32),
                        pltpu.VMEM((2, 1, tq), F32),
                        pltpu.VMEM((2, 1, tq), F32)],
        compiler_params=_cparams("parallel", "parallel", "arbitrary"),
        name="attention",
    )(lam, qt, k, vt, out_g)


def _row(t, j):
    return jnp.broadcast_to(t[j:j + 1, :], t.shape)


def _pair_masks(chunk, reverse):
    rows = lax.broadcasted_iota(jnp.int32, (chunk, chunk), 0)
    cols = lax.broadcasted_iota(jnp.int32, (chunk, chunk), 1)
    masks = []
    blk = SUBLANES
    while blk < chunk:
        if reverse:
            masks.append(((rows // blk) + 1 == (cols // blk)) & ((cols // blk) % 2 == 1))
        else:
            masks.append(((rows // blk) == (cols // blk) + 1) & ((rows // blk) % 2 == 1))
        blk *= 2
    return masks


def _hgrn_prep(q, z, lb, reverse):
    chunk = q.shape[0]
    n_tiles = chunk // SUBLANES
    e = jnp.exp(-jnp.abs(z))
    one_plus_e = 1.0 + e
    log_sig = jnp.minimum(z, 0.0) - jnp.log(one_plus_e)
    log_f = jnp.minimum(log_sig + jnp.log(1.0 + lb * jnp.exp(jnp.minimum(-z, EXP_CLAMP))), 0.0)
    k = (1.0 - lb) * (jnp.where(z >= 0, e, 1.0) / one_plus_e)

    sub = lax.broadcasted_iota(jnp.int32, (SUBLANES, LANES), 0)
    tiles = lambda a: [a[i * SUBLANES:(i + 1) * SUBLANES, :] for i in range(n_tiles)]
    q_t, k_t, lf_t = tiles(q), tiles(k), tiles(log_f)
    c_t = []
    for lf in lf_t:
        c = lf
        for sh in (1, 2, 4):
            if reverse:
                c = c + jnp.where(sub < SUBLANES - sh, pltpu.roll(c, SUBLANES - sh, axis=0), 0.0)
            else:
                c = c + jnp.where(sub >= sh, pltpu.roll(c, sh, axis=0), 0.0)
        c_t.append(c)
    return k, q_t, k_t, c_t


def _hgrn_finish(q, v, k, q_t, k_t, c_t, st_ref, base_mask, pair_masks, reverse, bounded):
    chunk = q.shape[0]
    n_tiles = chunk // SUBLANES
    sub = lax.broadcasted_iota(jnp.int32, (SUBLANES, LANES), 0)
    edge = 0 if reverse else SUBLANES - 1
    order = (lambda i: n_tiles - 1 - i) if reverse else (lambda i: i)
    nt_dims = (((1,), (1,)), ((), ()))

    def extend(c_t, m):
        out = []
        for i in range(n_tiles):
            pos = order(i)
            if (pos // m) % 2 == 1:
                last = (pos // m) * m - 1
                out.append(c_t[i] + _row(c_t[order(last)], edge))
            else:
                out.append(c_t[i])
        return out

    m = 1
    level = 0
    if bounded:
        while m * SUBLANES < min(HG_BASE_ROWS, chunk):
            c_t = extend(c_t, m)
            m *= 2
            level += 1
        qm = jnp.concatenate([q_t[i] * jnp.exp(c_t[i]) for i in range(n_tiles)], axis=0).astype(BF16)
        km = jnp.concatenate([k_t[i] * jnp.exp(-c_t[i]) for i in range(n_tiles)], axis=0).astype(BF16)
        sc = lax.dot_general(qm, km, nt_dims, preferred_element_type=F32)
        a_mat = jnp.where(base_mask, sc, 0.0)
    else:
        lane = lax.broadcasted_iota(jnp.int32, (SUBLANES, LANES), 1)
        a_tiles = []
        for i in range(n_tiles):
            a = jnp.zeros((SUBLANES, LANES), F32)
            for j in range(SUBLANES):
                valid = (sub <= j) if reverse else (sub >= j)
                term = jnp.where(valid, jnp.exp(c_t[i] - _row(c_t[i], j)) * q_t[i] * _row(k_t[i], j), 0.0)
                col = jnp.sum(term, axis=1, keepdims=True)
                a = jnp.where(lane == i * SUBLANES + j, col, a)
            a_tiles.append(a)
        a_mat = jnp.concatenate(a_tiles, axis=0)
        if chunk < LANES:
            a_mat = a_mat[:, :chunk]

    while m < n_tiles:
        qs, ks = [], []
        for i in range(n_tiles):
            pos = order(i)
            is_query = (pos // m) % 2 == 1
            if is_query:
                qs.append(q_t[i] * jnp.exp(c_t[i]))
                ks.append(jnp.zeros((SUBLANES, LANES), F32))
            else:
                last = (pos // m) * m + m - 1
                tot = _row(c_t[order(last)], edge)
                qs.append(jnp.zeros((SUBLANES, LANES), F32))
                ks.append(k_t[i] * jnp.exp(tot - c_t[i]))
        qm = jnp.concatenate(qs, axis=0).astype(BF16)
        km = jnp.concatenate(ks, axis=0).astype(BF16)
        sc = lax.dot_general(qm, km, nt_dims, preferred_element_type=F32)
        a_mat = jnp.where(pair_masks[level], sc, a_mat)
        c_t = extend(c_t, m)
        m *= 2
        level += 1

    total = _row(c_t[order(n_tiles - 1)], edge)
    c_full = jnp.concatenate(c_t, axis=0)
    q_in = (q * jnp.exp(c_full)).astype(BF16)
    k_out = (k * jnp.exp(jnp.concatenate([total] * n_tiles, axis=0) - c_full)).astype(BF16)
    v16 = v.astype(BF16)
    st = st_ref[...]
    o = (jnp.dot(a_mat.astype(BF16), v16, preferred_element_type=F32)
         + lax.dot_general(q_in, st.astype(BF16), nt_dims, preferred_element_type=F32))
    decay = jnp.exp(total[:1, :])
    st_ref[...] = st * decay + lax.dot_general(v16, k_out, (((0,), (0,)), ((), ())),
                                               preferred_element_type=F32)
    return o


def _hgrn_kernel(qf_ref, zf_ref, vf_ref, qb_ref, zb_ref, vb_ref, lb_ref, of_ref, ob_ref, st_ref):
    @pl.when(pl.program_id(1) == 0)
    def _():
        st_ref[...] = jnp.zeros(st_ref.shape, F32)

    c = HG_CHUNK
    base = min(HG_BASE_ROWS, c)
    n_chunks = qf_ref.shape[1] // c
    rows = lax.broadcasted_iota(jnp.int32, (c, c), 0)
    cols = lax.broadcasted_iota(jnp.int32, (c, c), 1)
    same_base = (rows // base) == (cols // base)
    dirs = ((qf_ref, zf_ref, vf_ref, of_ref, same_base & (cols <= rows), _pair_masks(c, reverse=False)),
            (qb_ref, zb_ref, vb_ref, ob_ref, same_base & (cols >= rows), _pair_masks(c, reverse=True)))

    def body(j, carry):
        row_sel = (pl.ds(pl.multiple_of(j * c, c), c),
                   pl.ds(pl.multiple_of((n_chunks - 1 - j) * c, c), c))
        low = jnp.zeros((1, HG_WIDTH), F32)
        for d in range(2):
            zneg = jnp.minimum(dirs[d][1][0, row_sel[d], :], 0.0)
            for blk in range(c // base):
                low = jnp.minimum(low, jnp.sum(zneg[blk * base:(blk + 1) * base, :], axis=0, keepdims=True))
        bounded = jnp.min(low) - base * math.log(2.0) >= -HG_BASE_BOUND

        def chunk_step(is_bounded):
            for hd in range(HG_HEADS):
                cs = slice(hd * HG_DIM, (hd + 1) * HG_DIM)
                for d, (q_ref, z_ref, v_ref, o_ref, base_mask, pair_masks) in enumerate(dirs):
                    q = q_ref[0, row_sel[d], cs]
                    k, q_t, k_t, c_t = _hgrn_prep(q, z_ref[0, row_sel[d], cs], lb_ref[d:d + 1, cs],
                                                  reverse=bool(d))
                    o_ref[0, row_sel[d], cs] = _hgrn_finish(
                        q, v_ref[0, row_sel[d], cs], k, q_t, k_t, c_t, st_ref.at[d, hd],
                        base_mask, pair_masks, reverse=bool(d), bounded=is_bounded)

        pl.when(bounded)(functools.partial(chunk_step, True))
        pl.when(jnp.logical_not(bounded))(functools.partial(chunk_step, False))
        return carry

    lax.fori_loop(0, n_chunks, body, 0)


def _hgrn(hg, lb):
    batch, seq, _ = hg.shape
    c = min(HG_BLOCK, seq)
    n = seq // c
    fwd = lambda col: pl.BlockSpec((1, c, HG_WIDTH), lambda b, j: (b, j, col))
    bwd = lambda col: pl.BlockSpec((1, c, HG_WIDTH), lambda b, j: (b, n - 1 - j, col))
    return pl.pallas_call(
        _hgrn_kernel,
        grid=(batch, n),
        in_specs=[fwd(0), fwd(1), fwd(3), bwd(0), bwd(2), bwd(3),
                  pl.BlockSpec((2, HG_WIDTH), lambda b, j: (0, 0))],
        out_specs=[fwd(0), bwd(0)],
        out_shape=[jax.ShapeDtypeStruct((batch, seq, HG_WIDTH), F32)] * 2,
        scratch_shapes=[pltpu.VMEM((2, HG_HEADS, HG_DIM, HG_DIM), F32)],
        compiler_params=_cparams("parallel", "arbitrary"),
        name="hgrn",
    )(hg, hg, hg, hg, hg, hg, lb)


def _out_proj_kernel(ya_ref, of_ref, ob_ref, gate_ref, x_ref, w_ref, hgg_ref, x_out):
    o = of_ref[...] + ob_ref[...]
    gate = gate_ref[...]
    parts = []
    for hd in range(HG_HEADS):
        cs = slice(hd * HG_DIM, (hd + 1) * HG_DIM)
        g = gate[:, cs]
        parts.append((_rms_rows(o[:, cs], hgg_ref[...]) * (g * jax.nn.sigmoid(g))).astype(BF16))
    y = jnp.concatenate([ya_ref[...]] + parts, axis=1)
    x_out[...] = x_ref[...] + jnp.dot(y, w_ref[...], preferred_element_type=F32)


def _out_proj(ya, o_f, o_b, hg, x2d, w_bf16, hg_out_g):
    rows, d_model = x2d.shape
    tm = PROJ_ROWS
    gate_col = hg.shape[1] // HG_WIDTH - 1
    row_blk = lambda cols: pl.BlockSpec((tm, cols), lambda i: (i, 0))
    return pl.pallas_call(
        _out_proj_kernel,
        grid=(rows // tm,),
        in_specs=[row_blk(DA_WIDTH), row_blk(HG_WIDTH), row_blk(HG_WIDTH),
                  pl.BlockSpec((tm, HG_WIDTH), lambda i: (i, gate_col)),
                  row_blk(d_model), _resident(w_bf16.shape), _resident((1, HG_DIM))],
        out_specs=row_blk(d_model),
        out_shape=jax.ShapeDtypeStruct((rows, d_model), F32),
        compiler_params=_cparams("parallel"),
        name="out_proj",
    )(ya, o_f, o_b, hg, x2d, w_bf16, hg_out_g)


def _ffn_kernel(x_ref, xp_ref, xn_ref, fg_ref, wup_ref, cw_ref, cb_ref, wdn_ref, o_ref,
                hext_ref, u_buf, g_ref, *, tiles_per_seq):
    tm = x_ref.shape[0]
    i = pl.program_id(0)
    first = i % tiles_per_seq == 0
    last = i % tiles_per_seq == tiles_per_seq - 1
    fg = fg_ref[...]
    pad = jnp.zeros((HALO - SUBLANES, x_ref.shape[1]), F32)
    hp = jnp.where(first, 0.0, _rms_rows(xp_ref[...], fg))
    hn = jnp.where(last, 0.0, _rms_rows(xn_ref[...], fg))
    hext_ref[:HALO] = jnp.concatenate([pad, hp], axis=0).astype(BF16)
    hext_ref[HALO:HALO + tm] = _rms_rows(x_ref[...], fg).astype(BF16)
    hext_ref[HALO + tm:] = jnp.concatenate([hn, pad], axis=0).astype(BF16)
    n_chunks = D_FF // FFN_COLS

    def up(c, slot):
        for part in range(2):
            lo = part * D_FF + c * FFN_COLS
            u_buf[slot, part] = jnp.dot(hext_ref[...], wup_ref[:, lo:lo + FFN_COLS],
                                        preferred_element_type=F32)

    def conv_gate(c, slot):
        outs = []
        for part in range(2):
            lo = part * D_FF + c * FFN_COLS
            w = cw_ref[:, lo:lo + FFN_COLS]
            outs.append(w[0:1] * u_buf[slot, part, pl.ds(HALO - 1, tm), :]
                        + w[1:2] * u_buf[slot, part, pl.ds(HALO, tm), :]
                        + w[2:3] * u_buf[slot, part, pl.ds(HALO + 1, tm), :]
                        + cb_ref[:, lo:lo + FFN_COLS])
        a, v = outs
        g_ref[:, c * FFN_COLS:(c + 1) * FFN_COLS] = (a * jax.nn.sigmoid(a) * v).astype(BF16)

    slice_ends = [min((s + 1) * FFN_DOWN_CHUNKS, n_chunks) for s in range(pl.cdiv(n_chunks, FFN_DOWN_CHUNKS))]
    done = 0
    up(0, 0)
    for c in range(n_chunks):
        if c + 1 < n_chunks:
            up(c + 1, (c + 1) % 2)
        conv_gate(c, c % 2)
        if c + 1 in slice_ends:
            lo, hi = done * FFN_COLS, (c + 1) * FFN_COLS
            part = jnp.dot(g_ref[:, lo:hi], wdn_ref[lo:hi, :], preferred_element_type=F32)
            o_ref[...] = (x_ref[...] if done == 0 else o_ref[...]) + part
            done = c + 1


def _ffn(x1, ffn_g, w_up, conv_w, conv_b, w_down, seq):
    rows, d_model = x1.shape
    tm = PROJ_ROWS
    tiles_per_seq = seq // tm
    per_tile = tm // SUBLANES
    n_halo_blocks = rows // SUBLANES
    kernel = functools.partial(_ffn_kernel, tiles_per_seq=tiles_per_seq)
    return pl.pallas_call(
        kernel,
        grid=(rows // tm,),
        in_specs=[pl.BlockSpec((tm, d_model), lambda i: (i, 0)),
                  pl.BlockSpec((SUBLANES, d_model), lambda i: (jnp.maximum(i * per_tile - 1, 0), 0)),
                  pl.BlockSpec((SUBLANES, d_model),
                               lambda i: (jnp.minimum((i + 1) * per_tile, n_halo_blocks - 1), 0)),
                  _resident((1, d_model)),
                  _resident(w_up.shape), _resident(conv_w.shape), _resident(conv_b.shape),
                  _resident(w_down.shape)],
        out_specs=pl.BlockSpec((tm, d_model), lambda i: (i, 0)),
        out_shape=jax.ShapeDtypeStruct((rows, d_model), F32),
        scratch_shapes=[pltpu.VMEM((tm + 2 * HALO, d_model), BF16),
                        pltpu.VMEM((2, 2, tm + 2 * HALO, FFN_COLS), F32),
                        pltpu.VMEM((tm, D_FF), BF16)],
        compiler_params=_cparams("parallel"),
        name="ffn",
    )(x1, x1, x1, ffn_g, w_up, conv_w, conv_b, w_down)


def _rope_tables(positions):
    half = DA_HEAD_DIM // 2
    inv_freq = ROPE_THETA ** (-jnp.arange(half, dtype=F32) / half)
    ang = positions.astype(F32)[:, :, None] * inv_freq
    cos, sin = jnp.cos(ang), jnp.sin(ang)
    reps = LANES // half
    cos_t = jnp.tile(cos, (1, 1, reps))
    sin_t = jnp.tile(jnp.concatenate([-sin, sin], axis=-1), (1, 1, reps // 2))
    return cos_t.reshape(-1, LANES), sin_t.reshape(-1, LANES)


def kernel(x, positions, mix_norm_g, w_in, q_norm_g, k_norm_g, lam_q1, lam_k1, lam_q2, lam_k2,
           diff_out_g, hg_lb_logits, hg_out_g, w_out, ffn_norm_g, w_up, conv_w, conv_b, w_down):
    batch, seq, d_model = x.shape
    assert seq % PROJ_ROWS == 0 and seq % ATTN_Q_COLS == 0 and seq % HG_BLOCK == 0
    cos_t, sin_t = _rope_tables(positions)
    p = jax.nn.softmax(hg_lb_logits.astype(F32), axis=0)
    lower_bounds = jnp.clip(jnp.cumsum(p, axis=0) - p[0:1], 0.0, 1.0 - 1e-4)
    gid = np.arange(DA_QK) // DA_HEAD_DIM
    gsum = jnp.asarray(gid[:, None] == gid[None, :], BF16)
    n_groups = DA_QK // DA_HEAD_DIM

    x2d = x.reshape(batch * seq, d_model)
    for l in range(DEPTH):
        lam_init = 0.8 - 0.6 * float(np.exp(-0.3 * l))
        lam = (jnp.exp(jnp.sum(lam_q1[l].astype(F32) * lam_k1[l].astype(F32)))
               - jnp.exp(jnp.sum(lam_q2[l].astype(F32) * lam_k2[l].astype(F32))) + lam_init)
        qt, k, vt, hg = _in_proj(
            x2d, mix_norm_g[l][None, :], w_in[l].astype(BF16), cos_t, sin_t,
            jnp.tile(q_norm_g[l], n_groups)[None, :], jnp.tile(k_norm_g[l], n_groups)[None, :],
            gsum, batch, seq)
        y_a = _attention(qt, k.reshape(batch, seq, DA_QK), vt,
                         lam.reshape(1, 1), diff_out_g[l][None, :], 1.0 - lam_init)
        o_f, o_b = _hgrn(hg.reshape(batch, seq, -1), lower_bounds[l])
        x1 = _out_proj(y_a.reshape(batch * seq, DA_WIDTH), o_f.reshape(batch * seq, HG_WIDTH),
                       o_b.reshape(batch * seq, HG_WIDTH), hg, x2d, w_out[l].astype(BF16),
                       hg_out_g[l][None, :])
        x2d = _ffn(x1, ffn_norm_g[l][None, :], w_up[l].astype(BF16), conv_w[l], conv_b[l][None, :],
                   w_down[l].astype(BF16), seq)
    return x2d.reshape(batch, seq, d_model)
```

```python
import functools
import math

import jax
import jax.numpy as jnp
import numpy as np
from jax import lax
from jax.experimental import pallas as pl
from jax.experimental.pallas import tpu as pltpu

F32 = jnp.float32
BF16 = jnp.bfloat16

DEPTH = 2
DA_HEADS = 4
DA_HEAD_DIM = 64
DA_V_DIM = 128
DA_QK = 512
DA_WIDTH = 512
HG_HEADS = 4
HG_DIM = 128
HG_WIDTH = 512
D_FF = 2816
ROPE_THETA = 10000.0
EPS = 1e-6
EXP_CLAMP = 30.0

LANES = 128
SUBLANES = 8
VMEM_LIMIT_BYTES = 56 * 1024 * 1024

PROJ_ROWS = 512
ATTN_Q_COLS = 2048
ATTN_K_ROWS = 512
ATTN_K_ROWS_GENERAL = 256
SCORE_BOUND = 20.0
HG_CHUNK = 64
HG_BLOCK = 512
HG_STEPS_PER_ITER = 4
HG_BASE_ROWS = 32
HG_BASE_BOUND = 60.0
FFN_COLS = 256
FFN_DOWN_CHUNKS = 11
HALO = 2 * SUBLANES


def _cparams(*sem):
    return pltpu.CompilerParams(dimension_semantics=sem, vmem_limit_bytes=VMEM_LIMIT_BYTES)


def _resident(shape):
    nd = len(shape)
    return pl.BlockSpec(shape, lambda *_: (0,) * nd, pipeline_mode=pl.Buffered(1))


def _rms_rows(x, g):
    ms = jnp.mean(x * x, axis=-1, keepdims=True)
    return x * lax.rsqrt(ms + EPS) * g


def _group_mean_sq(t, gsum):
    ss = jnp.dot((t * t).astype(BF16), gsum, preferred_element_type=F32)
    return ss * (1.0 / DA_HEAD_DIM)


def _rope(t, cos, sin_signed, first_half):
    width = t.shape[-1]
    half = DA_HEAD_DIM // 2
    swapped = jnp.where(first_half, pltpu.roll(t, width - half, axis=1), pltpu.roll(t, half, axis=1))
    return t * cos + swapped * sin_signed


def _in_proj_kernel(x_ref, g_ref, w_ref, cos_ref, sin_ref, qg_ref, kg_ref, gsum_ref,
                    qt_out, k_out, vt_out, hg_out):
    h = _rms_rows(x_ref[...], g_ref[...]).astype(BF16)

    def proj(lo, hi):
        return jnp.dot(h, w_ref[:, lo:hi], preferred_element_type=F32)

    cos = jnp.concatenate([cos_ref[...]] * (DA_QK // LANES), axis=1)
    sin = jnp.concatenate([sin_ref[...]] * (DA_QK // LANES), axis=1)
    lane = lax.broadcasted_iota(jnp.int32, (1, DA_QK), 1)
    first_half = (lane % DA_HEAD_DIM) < (DA_HEAD_DIM // 2)
    gsum = gsum_ref[...]

    q = proj(0, DA_QK)
    q = q * lax.rsqrt(_group_mean_sq(q, gsum) + EPS) * qg_ref[...]
    q = _rope(q, cos, sin, first_half) * (DA_HEAD_DIM ** -0.5)
    qt_out[0] = q.T.astype(BF16)

    k = proj(DA_QK, 2 * DA_QK)
    k = k * lax.rsqrt(_group_mean_sq(k, gsum) + EPS) * kg_ref[...]
    k_out[...] = _rope(k, cos, sin, first_half).astype(BF16)

    vt_out[0] = proj(2 * DA_QK, 2 * DA_QK + DA_WIDTH).T.astype(BF16)

    hg_out[...] = proj(2 * DA_QK + DA_WIDTH, w_ref.shape[1])


def _in_proj(x2d, g, w_bf16, cos_t, sin_t, qg, kg, gsum, batch, seq):
    rows, d_model = x2d.shape
    d_in = w_bf16.shape[1]
    tm = PROJ_ROWS
    tiles_per_seq = seq // tm
    hg_cols = d_in - 2 * DA_QK - DA_WIDTH
    row_blk = lambda cols: pl.BlockSpec((tm, cols), lambda i: (i, 0))
    t_blk = lambda ch: pl.BlockSpec((1, ch, tm), lambda i: (i // tiles_per_seq, 0, i % tiles_per_seq))
    return pl.pallas_call(
        _in_proj_kernel,
        grid=(rows // tm,),
        in_specs=[row_blk(d_model), _resident((1, d_model)), _resident((d_model, d_in)),
                  row_blk(LANES), row_blk(LANES), _resident((1, DA_QK)), _resident((1, DA_QK)),
                  _resident((DA_QK, DA_QK))],
        out_specs=[t_blk(DA_QK), row_blk(DA_QK), t_blk(DA_WIDTH), row_blk(hg_cols)],
        out_shape=[jax.ShapeDtypeStruct((batch, DA_QK, seq), BF16),
                   jax.ShapeDtypeStruct((rows, DA_QK), BF16),
                   jax.ShapeDtypeStruct((batch, DA_WIDTH, seq), BF16),
                   jax.ShapeDtypeStruct((rows, hg_cols), F32)],
        compiler_params=_cparams("parallel"),
        name="in_proj",
    )(x2d, g, w_bf16, cos_t, sin_t, qg, kg, gsum)


def _attn_kernel(lam_ref, bounded_ref, qt_ref, k_ref, vt_ref, og_ref, o_ref,
                 qh_ref, s_buf, p_buf, a_buf, acc_ref, m_ref, l_ref, *, out_scale):
    seq = k_ref.shape[1]
    tk = p_buf.shape[2]
    tk_gen = s_buf.shape[2]
    assert (seq // tk) % 2 == 0 and (seq // tk_gen) % 2 == 0
    qt = qt_ref[0]
    row = lax.broadcasted_iota(jnp.int32, (LANES, 1), 0)
    zero = jnp.zeros_like(qt)
    qh_ref[0] = jnp.where(row < DA_HEAD_DIM, qt, zero)
    qh_ref[1] = jnp.where(row >= DA_HEAD_DIM, qt, zero)
    acc_ref[...] = jnp.zeros(acc_ref.shape, F32)
    l_ref[...] = jnp.zeros(l_ref.shape, F32)

    def k_chunk(j, size):
        return k_ref[0, pl.ds(pl.multiple_of(j * size, size), size), :]

    def vt_chunk(j, size):
        return vt_ref[0, :, pl.ds(pl.multiple_of(j * size, size), size)]

    bounded = bounded_ref[0, 0] != 0

    @pl.when(bounded)
    def _():
        n_chunks = seq // tk

        def exp_scores(j, slot):
            kc = k_chunk(j, tk)
            for c in range(2):
                s = jnp.dot(kc, qh_ref[c], preferred_element_type=F32)
                p = jnp.exp(s)
                l_ref[c] += jnp.sum(p, axis=0, keepdims=True)
                p_buf[slot, c] = p.astype(BF16)

        def values(j, slot):
            vc = vt_chunk(j, tk)
            for c in range(2):
                acc_ref[c] += jnp.dot(vc, p_buf[slot, c], preferred_element_type=F32)

        exp_scores(0, 0)

        def body(g, carry):
            j = 2 * g
            exp_scores(j + 1, 1)
            values(j, 0)
            exp_scores(j + 2, 0)
            values(j + 1, 1)
            return carry

        lax.fori_loop(0, n_chunks // 2 - 1, body, 0)
        exp_scores(n_chunks - 1, 1)
        values(n_chunks - 2, 0)
        values(n_chunks - 1, 1)

    @pl.when(jnp.logical_not(bounded))
    def _():
        n_chunks = seq // tk_gen
        m_ref[...] = jnp.full(m_ref.shape, -jnp.inf, F32)

        def scores(j, slot):
            kc = k_chunk(j, tk_gen)
            for c in range(2):
                s_buf[slot, c] = jnp.dot(kc, qh_ref[c], preferred_element_type=F32)

        def softmax(slot):
            for c in range(2):
                s = s_buf[slot, c]
                m_prev = m_ref[c]
                m_new = jnp.maximum(m_prev, jnp.max(s, axis=0, keepdims=True))
                p = jnp.exp(s - m_new)
                alpha = jnp.exp(m_prev - m_new)
                l_ref[c] = alpha * l_ref[c] + jnp.sum(p, axis=0, keepdims=True)
                m_ref[c] = m_new
                p_buf[slot, c, :tk_gen, :] = p.astype(BF16)
                a_buf[slot, c] = alpha

        def values(j, slot):
            vc = vt_chunk(j, tk_gen)
            for c in range(2):
                acc_ref[c] = (a_buf[slot, c] * acc_ref[c]
                              + jnp.dot(vc, p_buf[slot, c, :tk_gen, :], preferred_element_type=F32))

        scores(0, 0)
        scores(1, 1)
        softmax(0)

        def body(g, carry):
            j = 2 * g
            scores(j, 0)
            softmax(1)
            values(j - 2, 0)
            scores(j + 1, 1)
            softmax(0)
            values(j - 1, 1)
            return carry

        lax.fori_loop(1, n_chunks // 2, body, 0)
        softmax(1)
        values(n_chunks - 2, 0)
        values(n_chunks - 1, 1)

    ot = acc_ref[0] / l_ref[0] - lam_ref[0, 0] * (acc_ref[1] / l_ref[1])
    o_ref[0] = (_rms_rows(ot.T, og_ref[...]) * out_scale).astype(BF16)


def _score_bound(q_gain, k_gain):
    return 1.01 * DA_HEAD_DIM ** 0.5 * jnp.max(jnp.abs(q_gain)) * jnp.max(jnp.abs(k_gain))


def _attention(qt, k, vt, lam, bounded, out_g, out_scale):
    batch, seq, _ = k.shape
    tq = min(ATTN_Q_COLS, seq)
    tk = min(ATTN_K_ROWS, seq // 2)
    tk_gen = min(ATTN_K_ROWS_GENERAL, seq // 2)
    kernel = functools.partial(_attn_kernel, out_scale=out_scale)
    return pl.pallas_call(
        kernel,
        grid=(batch, DA_HEADS, seq // tq),
        in_specs=[pl.BlockSpec(memory_space=pltpu.SMEM), pl.BlockSpec(memory_space=pltpu.SMEM),
                  pl.BlockSpec((1, LANES, tq), lambda b, h, i: (b, h, i)),
                  pl.BlockSpec((1, seq, LANES), lambda b, h, i: (b, 0, h)),
                  pl.BlockSpec((1, DA_V_DIM, seq), lambda b, h, i: (b, h, 0)),
                  pl.BlockSpec((1, DA_V_DIM), lambda b, h, i: (0, 0))],
        out_specs=pl.BlockSpec((1, tq, DA_V_DIM), lambda b, h, i: (b, i, h)),
        out_shape=jax.ShapeDtypeStruct((batch, seq, DA_WIDTH), BF16),
        scratch_shapes=[pltpu.VMEM((2, LANES, tq), BF16),
                        pltpu.VMEM((2, 2, tk_gen, tq), F32),
                        pltpu.VMEM((2, 2, tk, tq), BF16),
                        pltpu.VMEM((2, 2, 1, tq), F32),
                        pltpu.VMEM((2, DA_V_DIM, tq), F32),
                        pltpu.VMEM((2, 1, tq), F32),
                        pltpu.VMEM((2, 1, tq), F32)],
        compiler_params=_cparams("parallel", "parallel", "arbitrary"),
        name="attention",
    )(lam, bounded, qt, k, vt, out_g)


def _row(t, j):
    return jnp.broadcast_to(t[j:j + 1, :], t.shape)


def _pair_masks(chunk, reverse):
    rows = lax.broadcasted_iota(jnp.int32, (chunk, chunk), 0)
    cols = lax.broadcasted_iota(jnp.int32, (chunk, chunk), 1)
    masks = []
    blk = SUBLANES
    while blk < chunk:
        if reverse:
            masks.append(((rows // blk) + 1 == (cols // blk)) & ((cols // blk) % 2 == 1))
        else:
            masks.append(((rows // blk) == (cols // blk) + 1) & ((rows // blk) % 2 == 1))
        blk *= 2
    return masks


def _hgrn_prep(q, z, lb, reverse):
    chunk = q.shape[0]
    n_tiles = chunk // SUBLANES
    e = jnp.exp(-jnp.abs(z))
    one_plus_e = 1.0 + e
    log_sig = jnp.minimum(z, 0.0) - jnp.log(one_plus_e)
    log_f = jnp.minimum(log_sig + jnp.log(1.0 + lb * jnp.exp(jnp.minimum(-z, EXP_CLAMP))), 0.0)
    k = (1.0 - lb) * (jnp.where(z >= 0, e, 1.0) / one_plus_e)

    sub = lax.broadcasted_iota(jnp.int32, (SUBLANES, LANES), 0)
    tiles = lambda a: [a[i * SUBLANES:(i + 1) * SUBLANES, :] for i in range(n_tiles)]
    q_t, k_t, lf_t = tiles(q), tiles(k), tiles(log_f)
    c_t = []
    for lf in lf_t:
        c = lf
        for sh in (1, 2, 4):
            if reverse:
                c = c + jnp.where(sub < SUBLANES - sh, pltpu.roll(c, SUBLANES - sh, axis=0), 0.0)
            else:
                c = c + jnp.where(sub >= sh, pltpu.roll(c, sh, axis=0), 0.0)
        c_t.append(c)
    return k, q_t, k_t, c_t


def _hgrn_finish(q, v, k, q_t, k_t, c_t, st_ref, base_mask, pair_masks, reverse, bounded):
    chunk = q.shape[0]
    n_tiles = chunk // SUBLANES
    sub = lax.broadcasted_iota(jnp.int32, (SUBLANES, LANES), 0)
    edge = 0 if reverse else SUBLANES - 1
    order = (lambda i: n_tiles - 1 - i) if reverse else (lambda i: i)
    nt_dims = (((1,), (1,)), ((), ()))

    def extend(c_t, m):
        out = []
        for i in range(n_tiles):
            pos = order(i)
            if (pos // m) % 2 == 1:
                last = (pos // m) * m - 1
                out.append(c_t[i] + _row(c_t[order(last)], edge))
            else:
                out.append(c_t[i])
        return out

    m = 1
    level = 0
    if bounded:
        while m * SUBLANES < min(HG_BASE_ROWS, chunk):
            c_t = extend(c_t, m)
            m *= 2
            level += 1
        qm = jnp.concatenate([q_t[i] * jnp.exp(c_t[i]) for i in range(n_tiles)], axis=0).astype(BF16)
        km = jnp.concatenate([k_t[i] * jnp.exp(-c_t[i]) for i in range(n_tiles)], axis=0).astype(BF16)
        sc = lax.dot_general(qm, km, nt_dims, preferred_element_type=F32)
        a_mat = jnp.where(base_mask, sc, 0.0)
    else:
        lane = lax.broadcasted_iota(jnp.int32, (SUBLANES, LANES), 1)
        a_tiles = []
        for i in range(n_tiles):
            a = jnp.zeros((SUBLANES, LANES), F32)
            for j in range(SUBLANES):
                valid = (sub <= j) if reverse else (sub >= j)
                term = jnp.where(valid, jnp.exp(c_t[i] - _row(c_t[i], j)) * q_t[i] * _row(k_t[i], j), 0.0)
                col = jnp.sum(term, axis=1, keepdims=True)
                a = jnp.where(lane == i * SUBLANES + j, col, a)
            a_tiles.append(a)
        a_mat = jnp.concatenate(a_tiles, axis=0)
        if chunk < LANES:
            a_mat = a_mat[:, :chunk]

    while m < n_tiles:
        qs, ks = [], []
        for i in range(n_tiles):
            pos = order(i)
            is_query = (pos // m) % 2 == 1
            if is_query:
                qs.append(q_t[i] * jnp.exp(c_t[i]))
                ks.append(jnp.zeros((SUBLANES, LANES), F32))
            else:
                last = (pos // m) * m + m - 1
                tot = _row(c_t[order(last)], edge)
                qs.append(jnp.zeros((SUBLANES, LANES), F32))
                ks.append(k_t[i] * jnp.exp(tot - c_t[i]))
        qm = jnp.concatenate(qs, axis=0).astype(BF16)
        km = jnp.concatenate(ks, axis=0).astype(BF16)
        sc = lax.dot_general(qm, km, nt_dims, preferred_element_type=F32)
        a_mat = jnp.where(pair_masks[level], sc, a_mat)
        c_t = extend(c_t, m)
        m *= 2
        level += 1

    total = _row(c_t[order(n_tiles - 1)], edge)
    c_full = jnp.concatenate(c_t, axis=0)
    q_in = (q * jnp.exp(c_full)).astype(BF16)
    k_out = (k * jnp.exp(jnp.concatenate([total] * n_tiles, axis=0) - c_full)).astype(BF16)
    v16 = v.astype(BF16)
    st = st_ref[...]
    o = (jnp.dot(a_mat.astype(BF16), v16, preferred_element_type=F32)
         + lax.dot_general(q_in, st.astype(BF16), nt_dims, preferred_element_type=F32))
    decay = jnp.exp(total[:1, :])
    st_ref[...] = st * decay + lax.dot_general(v16, k_out, (((0,), (0,)), ((), ())),
                                               preferred_element_type=F32)
    return o


def _hgrn_kernel(qf_ref, zf_ref, vf_ref, qb_ref, zb_ref, vb_ref, lb_ref, of_ref, ob_ref, st_ref):
    @pl.when(pl.program_id(1) == 0)
    def _():
        st_ref[...] = jnp.zeros(st_ref.shape, F32)

    c = HG_CHUNK
    base = min(HG_BASE_ROWS, c)
    n_chunks = qf_ref.shape[1] // c
    rows = lax.broadcasted_iota(jnp.int32, (c, c), 0)
    cols = lax.broadcasted_iota(jnp.int32, (c, c), 1)
    same_base = (rows // base) == (cols // base)
    dirs = ((qf_ref, zf_ref, vf_ref, of_ref, same_base & (cols <= rows), _pair_masks(c, reverse=False)),
            (qb_ref, zb_ref, vb_ref, ob_ref, same_base & (cols >= rows), _pair_masks(c, reverse=True)))

    steps = HG_STEPS_PER_ITER
    assert n_chunks % steps == 0

    def body(g, carry):
        row_sels = []
        for s in range(steps):
            j = g * steps + s
            row_sels.append((pl.ds(pl.multiple_of(j * c, c), c),
                             pl.ds(pl.multiple_of((n_chunks - 1 - j) * c, c), c)))
        low = jnp.zeros((1, HG_WIDTH), F32)
        for row_sel in row_sels:
            for d in range(2):
                zneg = jnp.minimum(dirs[d][1][0, row_sel[d], :], 0.0)
                for blk in range(c // base):
                    low = jnp.minimum(low, jnp.sum(zneg[blk * base:(blk + 1) * base, :], axis=0, keepdims=True))
        bounded = jnp.min(low) - base * math.log(2.0) >= -HG_BASE_BOUND

        def chunk_steps(is_bounded):
            for row_sel in row_sels:
                for hd in range(HG_HEADS):
                    cs = slice(hd * HG_DIM, (hd + 1) * HG_DIM)
                    for d, (q_ref, z_ref, v_ref, o_ref, base_mask, pair_masks) in enumerate(dirs):
                        q = q_ref[0, row_sel[d], cs]
                        k, q_t, k_t, c_t = _hgrn_prep(q, z_ref[0, row_sel[d], cs], lb_ref[d:d + 1, cs],
                                                      reverse=bool(d))
                        o_ref[0, row_sel[d], cs] = _hgrn_finish(
                            q, v_ref[0, row_sel[d], cs], k, q_t, k_t, c_t, st_ref.at[d, hd],
                            base_mask, pair_masks, reverse=bool(d), bounded=is_bounded)

        pl.when(bounded)(functools.partial(chunk_steps, True))
        pl.when(jnp.logical_not(bounded))(functools.partial(chunk_steps, False))
        return carry

    lax.fori_loop(0, n_chunks // steps, body, 0)


def _hgrn(hg, lb):
    batch, seq, _ = hg.shape
    c = min(HG_BLOCK, seq)
    n = seq // c
    fwd = lambda col: pl.BlockSpec((1, c, HG_WIDTH), lambda b, j: (b, j, col))
    bwd = lambda col: pl.BlockSpec((1, c, HG_WIDTH), lambda b, j: (b, n - 1 - j, col))
    return pl.pallas_call(
        _hgrn_kernel,
        grid=(batch, n),
        in_specs=[fwd(0), fwd(1), fwd(3), bwd(0), bwd(2), bwd(3),
                  pl.BlockSpec((2, HG_WIDTH), lambda b, j: (0, 0))],
        out_specs=[fwd(0), bwd(0)],
        out_shape=[jax.ShapeDtypeStruct((batch, seq, HG_WIDTH), F32)] * 2,
        scratch_shapes=[pltpu.VMEM((2, HG_HEADS, HG_DIM, HG_DIM), F32)],
        compiler_params=_cparams("parallel", "arbitrary"),
        name="hgrn",
    )(hg, hg, hg, hg, hg, hg, lb)


def _out_proj_kernel(ya_ref, of_ref, ob_ref, gate_ref, x_ref, w_ref, hgg_ref, x_out):
    o = of_ref[...] + ob_ref[...]
    gate = gate_ref[...]
    parts = []
    for hd in range(HG_HEADS):
        cs = slice(hd * HG_DIM, (hd + 1) * HG_DIM)
        g = gate[:, cs]
        parts.append((_rms_rows(o[:, cs], hgg_ref[...]) * (g * jax.nn.sigmoid(g))).astype(BF16))
    y = jnp.concatenate([ya_ref[...]] + parts, axis=1)
    x_out[...] = x_ref[...] + jnp.dot(y, w_ref[...], preferred_element_type=F32)


def _out_proj(ya, o_f, o_b, hg, x2d, w_bf16, hg_out_g):
    rows, d_model = x2d.shape
    tm = PROJ_ROWS
    gate_col = hg.shape[1] // HG_WIDTH - 1
    row_blk = lambda cols: pl.BlockSpec((tm, cols), lambda i: (i, 0))
    return pl.pallas_call(
        _out_proj_kernel,
        grid=(rows // tm,),
        in_specs=[row_blk(DA_WIDTH), row_blk(HG_WIDTH), row_blk(HG_WIDTH),
                  pl.BlockSpec((tm, HG_WIDTH), lambda i: (i, gate_col)),
                  row_blk(d_model), _resident(w_bf16.shape), _resident((1, HG_DIM))],
        out_specs=row_blk(d_model),
        out_shape=jax.ShapeDtypeStruct((rows, d_model), F32),
        compiler_params=_cparams("parallel"),
        name="out_proj",
    )(ya, o_f, o_b, hg, x2d, w_bf16, hg_out_g)


def _ffn_kernel(x_ref, xp_ref, xn_ref, fg_ref, wup_ref, cw_ref, cb_ref, wdn_ref, o_ref,
                hext_ref, u_buf, g_ref, *, tiles_per_seq):
    tm = x_ref.shape[0]
    i = pl.program_id(0)
    first = i % tiles_per_seq == 0
    last = i % tiles_per_seq == tiles_per_seq - 1
    fg = fg_ref[...]
    pad = jnp.zeros((HALO - SUBLANES, x_ref.shape[1]), F32)
    hp = jnp.where(first, 0.0, _rms_rows(xp_ref[...], fg))
    hn = jnp.where(last, 0.0, _rms_rows(xn_ref[...], fg))
    hext_ref[:HALO] = jnp.concatenate([pad, hp], axis=0).astype(BF16)
    hext_ref[HALO:HALO + tm] = _rms_rows(x_ref[...], fg).astype(BF16)
    hext_ref[HALO + tm:] = jnp.concatenate([hn, pad], axis=0).astype(BF16)
    n_chunks = D_FF // FFN_COLS

    def up(c, slot):
        for part in range(2):
            lo = part * D_FF + c * FFN_COLS
            u_buf[slot, part] = jnp.dot(hext_ref[...], wup_ref[:, lo:lo + FFN_COLS],
                                        preferred_element_type=F32)

    def conv_gate(c, slot):
        outs = []
        for part in range(2):
            lo = part * D_FF + c * FFN_COLS
            w = cw_ref[:, lo:lo + FFN_COLS]
            outs.append(w[0:1] * u_buf[slot, part, pl.ds(HALO - 1, tm), :]
                        + w[1:2] * u_buf[slot, part, pl.ds(HALO, tm), :]
                        + w[2:3] * u_buf[slot, part, pl.ds(HALO + 1, tm), :]
                        + cb_ref[:, lo:lo + FFN_COLS])
        a, v = outs
        g_ref[:, c * FFN_COLS:(c + 1) * FFN_COLS] = (a * jax.nn.sigmoid(a) * v).astype(BF16)

    slice_ends = [min((s + 1) * FFN_DOWN_CHUNKS, n_chunks) for s in range(pl.cdiv(n_chunks, FFN_DOWN_CHUNKS))]
    done = 0
    up(0, 0)
    for c in range(n_chunks):
        if c + 1 < n_chunks:
            up(c + 1, (c + 1) % 2)
        conv_gate(c, c % 2)
        if c + 1 in slice_ends:
            lo, hi = done * FFN_COLS, (c + 1) * FFN_COLS
            part = jnp.dot(g_ref[:, lo:hi], wdn_ref[lo:hi, :], preferred_element_type=F32)
            o_ref[...] = (x_ref[...] if done == 0 else o_ref[...]) + part
            done = c + 1


def _ffn(x1, ffn_g, w_up, conv_w, conv_b, w_down, seq):
    rows, d_model = x1.shape
    tm = PROJ_ROWS
    tiles_per_seq = seq // tm
    per_tile = tm // SUBLANES
    n_halo_blocks = rows // SUBLANES
    kernel = functools.partial(_ffn_kernel, tiles_per_seq=tiles_per_seq)
    return pl.pallas_call(
        kernel,
        grid=(rows // tm,),
        in_specs=[pl.BlockSpec((tm, d_model), lambda i: (i, 0)),
                  pl.BlockSpec((SUBLANES, d_model), lambda i: (jnp.maximum(i * per_tile - 1, 0), 0)),
                  pl.BlockSpec((SUBLANES, d_model),
                               lambda i: (jnp.minimum((i + 1) * per_tile, n_halo_blocks - 1), 0)),
                  _resident((1, d_model)),
                  _resident(w_up.shape), _resident(conv_w.shape), _resident(conv_b.shape),
                  _resident(w_down.shape)],
        out_specs=pl.BlockSpec((tm, d_model), lambda i: (i, 0)),
        out_shape=jax.ShapeDtypeStruct((rows, d_model), F32),
        scratch_shapes=[pltpu.VMEM((tm + 2 * HALO, d_model), BF16),
                        pltpu.VMEM((2, 2, tm + 2 * HALO, FFN_COLS), F32),
                        pltpu.VMEM((tm, D_FF), BF16)],
        compiler_params=_cparams("parallel"),
        name="ffn",
    )(x1, x1, x1, ffn_g, w_up, conv_w, conv_b, w_down)


def _rope_tables(positions):
    half = DA_HEAD_DIM // 2
    inv_freq = ROPE_THETA ** (-jnp.arange(half, dtype=F32) / half)
    ang = positions.astype(F32)[:, :, None] * inv_freq
    cos, sin = jnp.cos(ang), jnp.sin(ang)
    reps = LANES // half
    cos_t = jnp.tile(cos, (1, 1, reps))
    sin_t = jnp.tile(jnp.concatenate([-sin, sin], axis=-1), (1, 1, reps // 2))
    return cos_t.reshape(-1, LANES), sin_t.reshape(-1, LANES)


def kernel(x, positions, mix_norm_g, w_in, q_norm_g, k_norm_g, lam_q1, lam_k1, lam_q2, lam_k2,
           diff_out_g, hg_lb_logits, hg_out_g, w_out, ffn_norm_g, w_up, conv_w, conv_b, w_down):
    batch, seq, d_model = x.shape
    assert seq % PROJ_ROWS == 0 and seq % ATTN_Q_COLS == 0 and seq % HG_BLOCK == 0
    cos_t, sin_t = _rope_tables(positions)
    p = jax.nn.softmax(hg_lb_logits.astype(F32), axis=0)
    lower_bounds = jnp.clip(jnp.cumsum(p, axis=0) - p[0:1], 0.0, 1.0 - 1e-4)
    gid = np.arange(DA_QK) // DA_HEAD_DIM
    gsum = jnp.asarray(gid[:, None] == gid[None, :], BF16)
    n_groups = DA_QK // DA_HEAD_DIM

    x2d = x.reshape(batch * seq, d_model)
    for l in range(DEPTH):
        lam_init = 0.8 - 0.6 * float(np.exp(-0.3 * l))
        lam = (jnp.exp(jnp.sum(lam_q1[l].astype(F32) * lam_k1[l].astype(F32)))
               - jnp.exp(jnp.sum(lam_q2[l].astype(F32) * lam_k2[l].astype(F32))) + lam_init)
        qt, k, vt, hg = _in_proj(
            x2d, mix_norm_g[l][None, :], w_in[l].astype(BF16), cos_t, sin_t,
            jnp.tile(q_norm_g[l], n_groups)[None, :], jnp.tile(k_norm_g[l], n_groups)[None, :],
            gsum, batch, seq)
        bounded = (_score_bound(q_norm_g[l], k_norm_g[l]) <= SCORE_BOUND).astype(jnp.int32)
        y_a = _attention(qt, k.reshape(batch, seq, DA_QK), vt, lam.reshape(1, 1), bounded.reshape(1, 1),
                         diff_out_g[l][None, :], 1.0 - lam_init)
        o_f, o_b = _hgrn(hg.reshape(batch, seq, -1), lower_bounds[l])
        x1 = _out_proj(y_a.reshape(batch * seq, DA_WIDTH), o_f.reshape(batch * seq, HG_WIDTH),
                       o_b.reshape(batch * seq, HG_WIDTH), hg, x2d, w_out[l].astype(BF16),
                       hg_out_g[l][None, :])
        x2d = _ffn(x1, ffn_norm_g[l][None, :], w_up[l].astype(BF16), conv_w[l], conv_b[l][None, :],
                   w_down[l].astype(BF16), seq)
    return x2d.reshape(batch, seq, d_model)
```

```python
import functools
import math

import jax
import jax.numpy as jnp
import numpy as np
from jax import lax
from jax.experimental import pallas as pl
from jax.experimental.pallas import tpu as pltpu

F32 = jnp.float32
BF16 = jnp.bfloat16

DEPTH = 2
DA_HEADS = 4
DA_HEAD_DIM = 64
DA_V_DIM = 128
DA_QK = 512
DA_WIDTH = 512
HG_HEADS = 4
HG_DIM = 128
HG_WIDTH = 512
D_FF = 2816
ROPE_THETA = 10000.0
EPS = 1e-6
EXP_CLAMP = 30.0

LANES = 128
SUBLANES = 8
VMEM_LIMIT_BYTES = 56 * 1024 * 1024

PROJ_ROWS = 512
ATTN_Q_COLS = 2048
ATTN_K_ROWS = 512
ATTN_K_ROWS_GENERAL = 256
SCORE_BOUND = 20.0
HG_CHUNK = 64
HG_BLOCK = 512
HG_STEPS_PER_ITER = 4
HG_BASE_ROWS = 32
HG_BASE_BOUND = 64.0
FFN_COLS = 256
HALO = 2 * SUBLANES


def _cparams(*sem):
    return pltpu.CompilerParams(dimension_semantics=sem, vmem_limit_bytes=VMEM_LIMIT_BYTES)


def _resident(shape):
    nd = len(shape)
    return pl.BlockSpec(shape, lambda *_: (0,) * nd, pipeline_mode=pl.Buffered(1))


def _layer_resident(stacked, layer):
    _, rows, cols = stacked.shape
    return pl.BlockSpec((None, rows, cols), lambda *_: (layer, 0, 0), pipeline_mode=pl.Buffered(1))


def _rms_rows(x, g):
    ms = jnp.mean(x * x, axis=-1, keepdims=True)
    return x * lax.rsqrt(ms + EPS) * g


def _group_mean_sq(t, gsum):
    ss = jnp.dot((t * t).astype(BF16), gsum, preferred_element_type=F32)
    return ss * (1.0 / DA_HEAD_DIM)


def _rope(t, cos, sin_signed, first_half):
    width = t.shape[-1]
    half = DA_HEAD_DIM // 2
    swapped = jnp.where(first_half, pltpu.roll(t, width - half, axis=1), pltpu.roll(t, half, axis=1))
    return t * cos + swapped * sin_signed


def _in_proj_kernel(x_ref, g_ref, w_ref, cs_ref, qg_ref, kg_ref, gsum_ref,
                    qt_out, k_out, vt_out, hg_out):
    h = _rms_rows(x_ref[...], g_ref[...]).astype(BF16)

    def proj(lo, hi):
        return jnp.dot(h, w_ref[:, lo:hi], preferred_element_type=F32)

    cs = cs_ref[...]
    sc = pltpu.roll(cs, DA_HEAD_DIM, axis=1)
    low_half = lax.broadcasted_iota(jnp.int32, (1, LANES), 1) < DA_HEAD_DIM
    cos = jnp.concatenate([jnp.where(low_half, cs, sc)] * (DA_QK // LANES), axis=1)
    sin = jnp.concatenate([jnp.where(low_half, sc, cs)] * (DA_QK // LANES), axis=1)
    lane = lax.broadcasted_iota(jnp.int32, (1, DA_QK), 1)
    first_half = (lane % DA_HEAD_DIM) < (DA_HEAD_DIM // 2)
    gsum = gsum_ref[...]

    q = proj(0, DA_QK)
    q = q * lax.rsqrt(_group_mean_sq(q, gsum) + EPS) * qg_ref[...]
    q = _rope(q, cos, sin, first_half) * (DA_HEAD_DIM ** -0.5)
    qt_out[0] = q.T.astype(BF16)

    k = proj(DA_QK, 2 * DA_QK)
    k = k * lax.rsqrt(_group_mean_sq(k, gsum) + EPS) * kg_ref[...]
    k_out[...] = _rope(k, cos, sin, first_half).astype(BF16)

    vt_out[0] = proj(2 * DA_QK, 2 * DA_QK + DA_WIDTH).T.astype(BF16)

    hg_out[...] = proj(2 * DA_QK + DA_WIDTH, w_ref.shape[1])


def _in_proj(x2d, g, w_stack, layer, rope_t, qg, kg, gsum, batch, seq):
    rows, d_model = x2d.shape
    d_in = w_stack.shape[2]
    tm = PROJ_ROWS
    tiles_per_seq = seq // tm
    hg_cols = d_in - 2 * DA_QK - DA_WIDTH
    row_blk = lambda cols: pl.BlockSpec((tm, cols), lambda i: (i, 0))
    t_blk = lambda ch: pl.BlockSpec((1, ch, tm), lambda i: (i // tiles_per_seq, 0, i % tiles_per_seq))
    return pl.pallas_call(
        _in_proj_kernel,
        grid=(rows // tm,),
        in_specs=[row_blk(d_model), _resident((1, d_model)), _layer_resident(w_stack, layer),
                  row_blk(LANES), _resident((1, DA_QK)), _resident((1, DA_QK)),
                  _resident((DA_QK, DA_QK))],
        out_specs=[t_blk(DA_QK), row_blk(DA_QK), t_blk(DA_WIDTH), row_blk(hg_cols)],
        out_shape=[jax.ShapeDtypeStruct((batch, DA_QK, seq), BF16),
                   jax.ShapeDtypeStruct((rows, DA_QK), BF16),
                   jax.ShapeDtypeStruct((batch, DA_WIDTH, seq), BF16),
                   jax.ShapeDtypeStruct((rows, hg_cols), F32)],
        compiler_params=_cparams("parallel"),
        name="in_proj",
    )(x2d, g, w_stack, rope_t, qg, kg, gsum)


def _attn_kernel(lam_ref, bounded_ref, qt_ref, k_ref, vt_ref, og_ref, o_ref,
                 qh_ref, s_buf, p_buf, a_buf, acc_ref, m_ref, l_ref, *, out_scale):
    seq = k_ref.shape[1]
    tk = p_buf.shape[2]
    tk_gen = s_buf.shape[2]
    assert (seq // tk) % 2 == 0 and (seq // tk_gen) % 2 == 0
    qt = qt_ref[0]
    row = lax.broadcasted_iota(jnp.int32, (LANES, 1), 0)
    zero = jnp.zeros_like(qt)
    qh_ref[0] = jnp.where(row < DA_HEAD_DIM, qt, zero)
    qh_ref[1] = jnp.where(row >= DA_HEAD_DIM, qt, zero)
    acc_ref[...] = jnp.zeros(acc_ref.shape, F32)
    l_ref[...] = jnp.zeros(l_ref.shape, F32)

    def k_chunk(j, size):
        return k_ref[0, pl.ds(pl.multiple_of(j * size, size), size), :]

    def vt_chunk(j, size):
        return vt_ref[0, :, pl.ds(pl.multiple_of(j * size, size), size)]

    bounded = bounded_ref[0, 0] != 0

    @pl.when(bounded)
    def _():
        n_chunks = seq // tk

        def exp_scores(j, slot):
            kc = k_chunk(j, tk)
            for c in range(2):
                s = jnp.dot(kc, qh_ref[c], preferred_element_type=F32)
                p = jnp.exp(s)
                l_ref[c] += jnp.sum(p, axis=0, keepdims=True)
                p_buf[slot, c] = p.astype(BF16)

        def values(j, slot):
            vc = vt_chunk(j, tk)
            for c in range(2):
                acc_ref[c] += jnp.dot(vc, p_buf[slot, c], preferred_element_type=F32)

        exp_scores(0, 0)

        def body(g, carry):
            j = 2 * g
            exp_scores(j + 1, 1)
            values(j, 0)
            exp_scores(j + 2, 0)
            values(j + 1, 1)
            return carry

        lax.fori_loop(0, n_chunks // 2 - 1, body, 0)
        exp_scores(n_chunks - 1, 1)
        values(n_chunks - 2, 0)
        values(n_chunks - 1, 1)

    @pl.when(jnp.logical_not(bounded))
    def _():
        n_chunks = seq // tk_gen
        m_ref[...] = jnp.full(m_ref.shape, -jnp.inf, F32)

        def scores(j, slot):
            kc = k_chunk(j, tk_gen)
            for c in range(2):
                s_buf[slot, c] = jnp.dot(kc, qh_ref[c], preferred_element_type=F32)

        def softmax(slot):
            for c in range(2):
                s = s_buf[slot, c]
                m_prev = m_ref[c]
                m_new = jnp.maximum(m_prev, jnp.max(s, axis=0, keepdims=True))
                p = jnp.exp(s - m_new)
                alpha = jnp.exp(m_prev - m_new)
                l_ref[c] = alpha * l_ref[c] + jnp.sum(p, axis=0, keepdims=True)
                m_ref[c] = m_new
                p_buf[slot, c, :tk_gen, :] = p.astype(BF16)
                a_buf[slot, c] = alpha

        def values(j, slot):
            vc = vt_chunk(j, tk_gen)
            for c in range(2):
                acc_ref[c] = (a_buf[slot, c] * acc_ref[c]
                              + jnp.dot(vc, p_buf[slot, c, :tk_gen, :], preferred_element_type=F32))

        scores(0, 0)
        scores(1, 1)
        softmax(0)

        def body(g, carry):
            j = 2 * g
            scores(j, 0)
            softmax(1)
            values(j - 2, 0)
            scores(j + 1, 1)
            softmax(0)
            values(j - 1, 1)
            return carry

        lax.fori_loop(1, n_chunks // 2, body, 0)
        softmax(1)
        values(n_chunks - 2, 0)
        values(n_chunks - 1, 1)

    ot = acc_ref[0] / l_ref[0] - lam_ref[0, 0] * (acc_ref[1] / l_ref[1])
    o_ref[0] = (_rms_rows(ot.T, og_ref[...]) * out_scale).astype(BF16)


def _score_bound(q_gain, k_gain):
    return 1.01 * DA_HEAD_DIM ** 0.5 * jnp.max(jnp.abs(q_gain)) * jnp.max(jnp.abs(k_gain))


def _attention(qt, k, vt, lam, bounded, out_g, out_scale):
    batch, seq, _ = k.shape
    tq = min(ATTN_Q_COLS, seq)
    tk = min(ATTN_K_ROWS, seq // 2)
    tk_gen = min(ATTN_K_ROWS_GENERAL, seq // 2)
    kernel = functools.partial(_attn_kernel, out_scale=out_scale)
    return pl.pallas_call(
        kernel,
        grid=(batch, DA_HEADS, seq // tq),
        in_specs=[pl.BlockSpec(memory_space=pltpu.SMEM), pl.BlockSpec(memory_space=pltpu.SMEM),
                  pl.BlockSpec((1, LANES, tq), lambda b, h, i: (b, h, i)),
                  pl.BlockSpec((1, seq, LANES), lambda b, h, i: (b, 0, h)),
                  pl.BlockSpec((1, DA_V_DIM, seq), lambda b, h, i: (b, h, 0)),
                  pl.BlockSpec((1, DA_V_DIM), lambda b, h, i: (0, 0))],
        out_specs=pl.BlockSpec((1, tq, DA_V_DIM), lambda b, h, i: (b, i, h)),
        out_shape=jax.ShapeDtypeStruct((batch, seq, DA_WIDTH), BF16),
        scratch_shapes=[pltpu.VMEM((2, LANES, tq), BF16),
                        pltpu.VMEM((2, 2, tk_gen, tq), F32),
                        pltpu.VMEM((2, 2, tk, tq), BF16),
                        pltpu.VMEM((2, 2, 1, tq), F32),
                        pltpu.VMEM((2, DA_V_DIM, tq), F32),
                        pltpu.VMEM((2, 1, tq), F32),
                        pltpu.VMEM((2, 1, tq), F32)],
        compiler_params=_cparams("parallel", "parallel", "arbitrary"),
        name="attention",
    )(lam, bounded, qt, k, vt, out_g)


def _row(t, j):
    return jnp.broadcast_to(t[j:j + 1, :], t.shape)


def _pair_masks(chunk, reverse):
    rows = lax.broadcasted_iota(jnp.int32, (chunk, chunk), 0)
    cols = lax.broadcasted_iota(jnp.int32, (chunk, chunk), 1)
    masks = []
    blk = SUBLANES
    while blk < chunk:
        if reverse:
            masks.append(((rows // blk) + 1 == (cols // blk)) & ((cols // blk) % 2 == 1))
        else:
            masks.append(((rows // blk) == (cols // blk) + 1) & ((rows // blk) % 2 == 1))
        blk *= 2
    return masks


def _hgrn_prep(q, z, lb, reverse):
    chunk = q.shape[0]
    n_tiles = chunk // SUBLANES
    e = jnp.exp(-jnp.abs(z))
    one_plus_e = 1.0 + e
    log_sig = jnp.minimum(z, 0.0) - jnp.log(one_plus_e)
    log_f = jnp.minimum(log_sig + jnp.log(1.0 + lb * jnp.exp(jnp.minimum(-z, EXP_CLAMP))), 0.0)
    k = (1.0 - lb) * (jnp.where(z >= 0, e, 1.0) / one_plus_e)

    sub = lax.broadcasted_iota(jnp.int32, (SUBLANES, LANES), 0)
    tiles = lambda a: [a[i * SUBLANES:(i + 1) * SUBLANES, :] for i in range(n_tiles)]
    q_t, k_t, lf_t = tiles(q), tiles(k), tiles(log_f)
    c_t = []
    for lf in lf_t:
        c = lf
        for sh in (1, 2, 4):
            if reverse:
                c = c + jnp.where(sub < SUBLANES - sh, pltpu.roll(c, SUBLANES - sh, axis=0), 0.0)
            else:
                c = c + jnp.where(sub >= sh, pltpu.roll(c, sh, axis=0), 0.0)
        c_t.append(c)
    return k, q_t, k_t, c_t


def _hgrn_finish(q, v, k, q_t, k_t, c_t, st_ref, base_mask, pair_masks, reverse, bounded):
    chunk = q.shape[0]
    n_tiles = chunk // SUBLANES
    sub = lax.broadcasted_iota(jnp.int32, (SUBLANES, LANES), 0)
    edge = 0 if reverse else SUBLANES - 1
    order = (lambda i: n_tiles - 1 - i) if reverse else (lambda i: i)
    nt_dims = (((1,), (1,)), ((), ()))

    def extend(c_t, m):
        out = []
        for i in range(n_tiles):
            pos = order(i)
            if (pos // m) % 2 == 1:
                last = (pos // m) * m - 1
                out.append(c_t[i] + _row(c_t[order(last)], edge))
            else:
                out.append(c_t[i])
        return out

    m = 1
    level = 0
    if bounded:
        while m * SUBLANES < min(HG_BASE_ROWS, chunk):
            c_t = extend(c_t, m)
            m *= 2
            level += 1
        qm = jnp.concatenate([q_t[i] * jnp.exp(c_t[i]) for i in range(n_tiles)], axis=0).astype(BF16)
        km = jnp.concatenate([k_t[i] * jnp.exp(-c_t[i]) for i in range(n_tiles)], axis=0).astype(BF16)
        sc = lax.dot_general(qm, km, nt_dims, preferred_element_type=F32)
        a_mat = jnp.where(base_mask, sc, 0.0)
    else:
        lane = lax.broadcasted_iota(jnp.int32, (SUBLANES, LANES), 1)
        a_tiles = []
        for i in range(n_tiles):
            a = jnp.zeros((SUBLANES, LANES), F32)
            for j in range(SUBLANES):
                valid = (sub <= j) if reverse else (sub >= j)
                term = jnp.where(valid, jnp.exp(c_t[i] - _row(c_t[i], j)) * q_t[i] * _row(k_t[i], j), 0.0)
                col = jnp.sum(term, axis=1, keepdims=True)
                a = jnp.where(lane == i * SUBLANES + j, col, a)
            a_tiles.append(a)
        a_mat = jnp.concatenate(a_tiles, axis=0)
        if chunk < LANES:
            a_mat = a_mat[:, :chunk]

    while m < n_tiles:
        qs, ks = [], []
        for i in range(n_tiles):
            pos = order(i)
            is_query = (pos // m) % 2 == 1
            if is_query:
                qs.append(q_t[i] * jnp.exp(c_t[i]))
                ks.append(jnp.zeros((SUBLANES, LANES), F32))
            else:
                last = (pos // m) * m + m - 1
                tot = _row(c_t[order(last)], edge)
                qs.append(jnp.zeros((SUBLANES, LANES), F32))
                ks.append(k_t[i] * jnp.exp(tot - c_t[i]))
        qm = jnp.concatenate(qs, axis=0).astype(BF16)
        km = jnp.concatenate(ks, axis=0).astype(BF16)
        sc = lax.dot_general(qm, km, nt_dims, preferred_element_type=F32)
        a_mat = jnp.where(pair_masks[level], sc, a_mat)
        c_t = extend(c_t, m)
        m *= 2
        level += 1

    total = _row(c_t[order(n_tiles - 1)], edge)
    c_full = jnp.concatenate(c_t, axis=0)
    q_in = (q * jnp.exp(c_full)).astype(BF16)
    k_out = (k * jnp.exp(jnp.concatenate([total] * n_tiles, axis=0) - c_full)).astype(BF16)
    v16 = v.astype(BF16)
    st = st_ref[...]
    o = (jnp.dot(a_mat.astype(BF16), v16, preferred_element_type=F32)
         + lax.dot_general(q_in, st.astype(BF16), nt_dims, preferred_element_type=F32))
    decay = jnp.exp(total[:1, :])
    st_ref[...] = st * decay + lax.dot_general(v16, k_out, (((0,), (0,)), ((), ())),
                                               preferred_element_type=F32)
    return o


def _hgrn_kernel(qf_ref, zf_ref, vf_ref, qb_ref, zb_ref, vb_ref, lb_ref, of_ref, ob_ref, st_ref):
    @pl.when(pl.program_id(1) == 0)
    def _():
        st_ref[...] = jnp.zeros(st_ref.shape, F32)

    c = HG_CHUNK
    base = min(HG_BASE_ROWS, c)
    n_chunks = qf_ref.shape[1] // c
    rows = lax.broadcasted_iota(jnp.int32, (c, c), 0)
    cols = lax.broadcasted_iota(jnp.int32, (c, c), 1)
    same_base = (rows // base) == (cols // base)
    dirs = ((qf_ref, zf_ref, vf_ref, of_ref, same_base & (cols <= rows), _pair_masks(c, reverse=False)),
            (qb_ref, zb_ref, vb_ref, ob_ref, same_base & (cols >= rows), _pair_masks(c, reverse=True)))

    steps = HG_STEPS_PER_ITER
    assert n_chunks % steps == 0

    def body(g, carry):
        row_sels = []
        for s in range(steps):
            j = g * steps + s
            row_sels.append((pl.ds(pl.multiple_of(j * c, c), c),
                             pl.ds(pl.multiple_of((n_chunks - 1 - j) * c, c), c)))
        low = jnp.zeros((1, HG_WIDTH), F32)
        for row_sel in row_sels:
            for d in range(2):
                zneg = jnp.minimum(dirs[d][1][0, row_sel[d], :], 0.0)
                for blk in range(c // base):
                    low = jnp.minimum(low, jnp.sum(zneg[blk * base:(blk + 1) * base, :], axis=0, keepdims=True))
        bounded = jnp.min(low) - base * math.log(2.0) >= -HG_BASE_BOUND

        def chunk_steps(is_bounded):
            for row_sel in row_sels:
                for hd in range(HG_HEADS):
                    cs = slice(hd * HG_DIM, (hd + 1) * HG_DIM)
                    for d, (q_ref, z_ref, v_ref, o_ref, base_mask, pair_masks) in enumerate(dirs):
                        q = q_ref[0, row_sel[d], cs]
                        k, q_t, k_t, c_t = _hgrn_prep(q, z_ref[0, row_sel[d], cs], lb_ref[d:d + 1, cs],
                                                      reverse=bool(d))
                        o_ref[0, row_sel[d], cs] = _hgrn_finish(
                            q, v_ref[0, row_sel[d], cs], k, q_t, k_t, c_t, st_ref.at[d, hd],
                            base_mask, pair_masks, reverse=bool(d), bounded=is_bounded)

        pl.when(bounded)(functools.partial(chunk_steps, True))
        pl.when(jnp.logical_not(bounded))(functools.partial(chunk_steps, False))
        return carry

    lax.fori_loop(0, n_chunks // steps, body, 0)


def _hgrn(hg, lb):
    batch, seq, _ = hg.shape
    c = min(HG_BLOCK, seq)
    n = seq // c
    fwd = lambda col: pl.BlockSpec((1, c, HG_WIDTH), lambda b, j: (b, j, col))
    bwd = lambda col: pl.BlockSpec((1, c, HG_WIDTH), lambda b, j: (b, n - 1 - j, col))
    return pl.pallas_call(
        _hgrn_kernel,
        grid=(batch, n),
        in_specs=[fwd(0), fwd(1), fwd(3), bwd(0), bwd(2), bwd(3),
                  pl.BlockSpec((2, HG_WIDTH), lambda b, j: (0, 0))],
        out_specs=[fwd(0), bwd(0)],
        out_shape=[jax.ShapeDtypeStruct((batch, seq, HG_WIDTH), F32)] * 2,
        scratch_shapes=[pltpu.VMEM((2, HG_HEADS, HG_DIM, HG_DIM), F32)],
        compiler_params=_cparams("parallel", "arbitrary"),
        name="hgrn",
    )(hg, hg, hg, hg, hg, hg, lb)


def _out_proj_kernel(ya_ref, of_ref, ob_ref, gate_ref, x_ref, w_ref, hgg_ref, x_out):
    o = of_ref[...] + ob_ref[...]
    gate = gate_ref[...]
    parts = []
    for hd in range(HG_HEADS):
        cs = slice(hd * HG_DIM, (hd + 1) * HG_DIM)
        g = gate[:, cs]
        parts.append((_rms_rows(o[:, cs], hgg_ref[...]) * (g * jax.nn.sigmoid(g))).astype(BF16))
    y = jnp.concatenate([ya_ref[...]] + parts, axis=1)
    x_out[...] = x_ref[...] + jnp.dot(y, w_ref[...], preferred_element_type=F32)


def _out_proj(ya, o_f, o_b, hg, x2d, w_stack, layer, hg_out_g):
    rows, d_model = x2d.shape
    tm = PROJ_ROWS
    gate_col = hg.shape[1] // HG_WIDTH - 1
    row_blk = lambda cols: pl.BlockSpec((tm, cols), lambda i: (i, 0))
    return pl.pallas_call(
        _out_proj_kernel,
        grid=(rows // tm,),
        in_specs=[row_blk(DA_WIDTH), row_blk(HG_WIDTH), row_blk(HG_WIDTH),
                  pl.BlockSpec((tm, HG_WIDTH), lambda i: (i, gate_col)),
                  row_blk(d_model), _layer_resident(w_stack, layer), _resident((1, HG_DIM))],
        out_specs=row_blk(d_model),
        out_shape=jax.ShapeDtypeStruct((rows, d_model), F32),
        compiler_params=_cparams("parallel"),
        name="out_proj",
    )(ya, o_f, o_b, hg, x2d, w_stack, hg_out_g)


def _ffn_kernel(x_ref, xp_ref, xn_ref, fg_ref, wup_ref, cw_ref, cb_ref, wdn_ref, o_ref,
                hext_ref, u_buf, g_ref, *, tiles_per_seq):
    tm = x_ref.shape[0]
    i = pl.program_id(0)
    first = i % tiles_per_seq == 0
    last = i % tiles_per_seq == tiles_per_seq - 1
    fg = fg_ref[...]
    pad = jnp.zeros((HALO - SUBLANES, x_ref.shape[1]), F32)
    hp = jnp.where(first, 0.0, _rms_rows(xp_ref[...], fg))
    hn = jnp.where(last, 0.0, _rms_rows(xn_ref[...], fg))
    hext_ref[:HALO] = jnp.concatenate([pad, hp], axis=0).astype(BF16)
    hext_ref[HALO:HALO + tm] = _rms_rows(x_ref[...], fg).astype(BF16)
    hext_ref[HALO + tm:] = jnp.concatenate([hn, pad], axis=0).astype(BF16)
    n_chunks = D_FF // FFN_COLS

    def cols(c, part):
        return slice(part * D_FF + c * FFN_COLS, part * D_FF + (c + 1) * FFN_COLS)

    def up(c, slot):
        for part in range(2):
            u_buf[slot, part] = jnp.dot(hext_ref[...], wup_ref[:, cols(c, part)],
                                        preferred_element_type=F32)

    def conv_gate(c, slot):
        outs = []
        for part in range(2):
            w = cw_ref[:, cols(c, part)]
            outs.append(w[0:1] * u_buf[slot, part, pl.ds(HALO - 1, tm), :]
                        + w[1:2] * u_buf[slot, part, pl.ds(HALO, tm), :]
                        + w[2:3] * u_buf[slot, part, pl.ds(HALO + 1, tm), :]
                        + cb_ref[:, cols(c, part)])
        a, v = outs
        g_ref[:, c * FFN_COLS:(c + 1) * FFN_COLS] = (a * jax.nn.sigmoid(a) * v).astype(BF16)

    n_slots = u_buf.shape[0]
    up(0, 0)
    for c in range(n_chunks):
        if c + 1 < n_chunks:
            up(c + 1, (c + 1) % n_slots)
        conv_gate(c, c % n_slots)
    o_ref[...] = x_ref[...] + jnp.dot(g_ref[...], wdn_ref[...], preferred_element_type=F32)


def _ffn(x1, ffn_g, w_up_stack, conv_w, conv_b, w_down_stack, layer, seq):
    rows, d_model = x1.shape
    tm = PROJ_ROWS
    tiles_per_seq = seq // tm
    per_tile = tm // SUBLANES
    n_halo_blocks = rows // SUBLANES
    kernel = functools.partial(_ffn_kernel, tiles_per_seq=tiles_per_seq)
    return pl.pallas_call(
        kernel,
        grid=(rows // tm,),
        in_specs=[pl.BlockSpec((tm, d_model), lambda i: (i, 0)),
                  pl.BlockSpec((SUBLANES, d_model), lambda i: (jnp.maximum(i * per_tile - 1, 0), 0)),
                  pl.BlockSpec((SUBLANES, d_model),
                               lambda i: (jnp.minimum((i + 1) * per_tile, n_halo_blocks - 1), 0)),
                  _resident((1, d_model)),
                  _layer_resident(w_up_stack, layer), _resident(conv_w.shape), _resident(conv_b.shape),
                  _layer_resident(w_down_stack, layer)],
        out_specs=pl.BlockSpec((tm, d_model), lambda i: (i, 0)),
        out_shape=jax.ShapeDtypeStruct((rows, d_model), F32),
        scratch_shapes=[pltpu.VMEM((tm + 2 * HALO, d_model), BF16),
                        pltpu.VMEM((2, 2, tm + 2 * HALO, FFN_COLS), F32),
                        pltpu.VMEM((tm, D_FF), BF16)],
        compiler_params=_cparams("parallel"),
        name="ffn",
    )(x1, x1, x1, ffn_g, w_up_stack, conv_w, conv_b, w_down_stack)


def _rope_tables(positions):
    half = DA_HEAD_DIM // 2
    inv_freq = ROPE_THETA ** (-jnp.arange(half, dtype=F32) / half)
    ang = positions.astype(F32)[:, :, None] * inv_freq
    cos, sin = jnp.cos(ang), jnp.sin(ang)
    assert LANES == 4 * half
    return jnp.concatenate([cos, cos, -sin, sin], axis=-1).reshape(-1, LANES)


def kernel(x, positions, mix_norm_g, w_in, q_norm_g, k_norm_g, lam_q1, lam_k1, lam_q2, lam_k2,
           diff_out_g, hg_lb_logits, hg_out_g, w_out, ffn_norm_g, w_up, conv_w, conv_b, w_down):
    batch, seq, d_model = x.shape
    assert seq % PROJ_ROWS == 0 and seq % ATTN_Q_COLS == 0 and seq % HG_BLOCK == 0
    rope_t = _rope_tables(positions)
    p = jax.nn.softmax(hg_lb_logits.astype(F32), axis=0)
    lower_bounds = jnp.clip(jnp.cumsum(p, axis=0) - p[0:1], 0.0, 1.0 - 1e-4)
    gid = np.arange(DA_QK) // DA_HEAD_DIM
    gsum = jnp.asarray(gid[:, None] == gid[None, :], BF16)
    n_groups = DA_QK // DA_HEAD_DIM

    w_in16, w_out16, w_up16, w_down16 = (w.astype(BF16) for w in (w_in, w_out, w_up, w_down))

    x2d = x.reshape(batch * seq, d_model)
    for l in range(DEPTH):
        lam_init = 0.8 - 0.6 * float(np.exp(-0.3 * l))
        lam = (jnp.exp(jnp.sum(lam_q1[l].astype(F32) * lam_k1[l].astype(F32)))
               - jnp.exp(jnp.sum(lam_q2[l].astype(F32) * lam_k2[l].astype(F32))) + lam_init)
        qt, k, vt, hg = _in_proj(
            x2d, mix_norm_g[l][None, :], w_in16, l, rope_t,
            jnp.tile(q_norm_g[l], n_groups)[None, :], jnp.tile(k_norm_g[l], n_groups)[None, :],
            gsum, batch, seq)
        bounded = (_score_bound(q_norm_g[l], k_norm_g[l]) <= SCORE_BOUND).astype(jnp.int32)
        y_a = _attention(qt, k.reshape(batch, seq, DA_QK), vt, lam.reshape(1, 1), bounded.reshape(1, 1),
                         diff_out_g[l][None, :], 1.0 - lam_init)
        o_f, o_b = _hgrn(hg.reshape(batch, seq, -1), lower_bounds[l])
        x1 = _out_proj(y_a.reshape(batch * seq, DA_WIDTH), o_f.reshape(batch * seq, HG_WIDTH),
                       o_b.reshape(batch * seq, HG_WIDTH), hg, x2d, w_out16, l,
                       hg_out_g[l][None, :])
        x2d = _ffn(x1, ffn_norm_g[l][None, :], w_up16, conv_w[l], conv_b[l][None, :],
                   w_down16, l, seq)
    return x2d.reshape(batch, seq, d_model)
```

```python
import functools
import math

import jax
import jax.numpy as jnp
import numpy as np
from jax import lax
from jax.experimental import pallas as pl
from jax.experimental.pallas import tpu as pltpu

F32 = jnp.float32
BF16 = jnp.bfloat16

DEPTH = 2
DA_HEADS = 4
DA_HEAD_DIM = 64
DA_V_DIM = 128
DA_QK = 512
DA_WIDTH = 512
HG_HEADS = 4
HG_DIM = 128
HG_WIDTH = 512
D_FF = 2816
ROPE_THETA = 10000.0
EPS = 1e-6
EXP_CLAMP = 30.0

LANES = 128
SUBLANES = 8
VMEM_LIMIT_BYTES = 56 * 1024 * 1024

PROJ_ROWS = 512
ATTN_Q_COLS = 2048
ATTN_K_ROWS = 512
ATTN_K_ROWS_GENERAL = 256
SCORE_BOUND = 20.0
HG_CHUNK = 64
HG_BLOCK = 512
HG_STEPS_PER_ITER = 4
HG_BASE_ROWS = 32
HG_BASE_BOUND = 64.0
FFN_COLS = 256
HALO = 2 * SUBLANES


def _cparams(*sem):
    return pltpu.CompilerParams(dimension_semantics=sem, vmem_limit_bytes=VMEM_LIMIT_BYTES)


def _resident(shape):
    nd = len(shape)
    return pl.BlockSpec(shape, lambda *_: (0,) * nd, pipeline_mode=pl.Buffered(1))


def _layer_resident(stacked, layer):
    _, rows, cols = stacked.shape
    return pl.BlockSpec((None, rows, cols), lambda *_: (layer, 0, 0), pipeline_mode=pl.Buffered(1))


def _rms_rows(x, g):
    ms = jnp.mean(x * x, axis=-1, keepdims=True)
    return x * lax.rsqrt(ms + EPS) * g


def _group_mean_sq(t, gsum):
    ss = jnp.dot((t * t).astype(BF16), gsum, preferred_element_type=F32)
    return ss * (1.0 / DA_HEAD_DIM)


def _rope(t, cos, sin_signed, first_half):
    width = t.shape[-1]
    half = DA_HEAD_DIM // 2
    swapped = jnp.where(first_half, pltpu.roll(t, width - half, axis=1), pltpu.roll(t, half, axis=1))
    return t * cos + swapped * sin_signed


def _in_proj_kernel(x_ref, g_ref, w_ref, cs_ref, qg_ref, kg_ref, gsum_ref,
                    qt_out, k_out, vt_out, hg_out):
    h = _rms_rows(x_ref[...], g_ref[...]).astype(BF16)

    def proj(lo, hi):
        return jnp.dot(h, w_ref[:, lo:hi], preferred_element_type=F32)

    cs = cs_ref[...]
    sc = pltpu.roll(cs, DA_HEAD_DIM, axis=1)
    low_half = lax.broadcasted_iota(jnp.int32, (1, LANES), 1) < DA_HEAD_DIM
    cos = jnp.concatenate([jnp.where(low_half, cs, sc)] * (DA_QK // LANES), axis=1)
    sin = jnp.concatenate([jnp.where(low_half, sc, cs)] * (DA_QK // LANES), axis=1)
    lane = lax.broadcasted_iota(jnp.int32, (1, DA_QK), 1)
    first_half = (lane % DA_HEAD_DIM) < (DA_HEAD_DIM // 2)
    gsum = gsum_ref[...]

    q = proj(0, DA_QK)
    q = q * lax.rsqrt(_group_mean_sq(q, gsum) + EPS) * qg_ref[...]
    q = _rope(q, cos, sin, first_half) * (DA_HEAD_DIM ** -0.5)
    qt_out[0] = q.T.astype(BF16)

    k = proj(DA_QK, 2 * DA_QK)
    k = k * lax.rsqrt(_group_mean_sq(k, gsum) + EPS) * kg_ref[...]
    k_out[...] = _rope(k, cos, sin, first_half).astype(BF16)

    vt_out[0] = proj(2 * DA_QK, 2 * DA_QK + DA_WIDTH).T.astype(BF16)

    hg_out[...] = proj(2 * DA_QK + DA_WIDTH, w_ref.shape[1])


def _in_proj(x2d, g, w_stack, layer, rope_t, qg, kg, gsum, batch, seq):
    rows, d_model = x2d.shape
    d_in = w_stack.shape[2]
    tm = PROJ_ROWS
    tiles_per_seq = seq // tm
    hg_cols = d_in - 2 * DA_QK - DA_WIDTH
    row_blk = lambda cols: pl.BlockSpec((tm, cols), lambda i: (i, 0))
    t_blk = lambda ch: pl.BlockSpec((1, ch, tm), lambda i: (i // tiles_per_seq, 0, i % tiles_per_seq))
    return pl.pallas_call(
        _in_proj_kernel,
        grid=(rows // tm,),
        in_specs=[row_blk(d_model), _resident((1, d_model)), _layer_resident(w_stack, layer),
                  row_blk(LANES), _resident((1, DA_QK)), _resident((1, DA_QK)),
                  _resident((DA_QK, DA_QK))],
        out_specs=[t_blk(DA_QK), row_blk(DA_QK), t_blk(DA_WIDTH), row_blk(hg_cols)],
        out_shape=[jax.ShapeDtypeStruct((batch, DA_QK, seq), BF16),
                   jax.ShapeDtypeStruct((rows, DA_QK), BF16),
                   jax.ShapeDtypeStruct((batch, DA_WIDTH, seq), BF16),
                   jax.ShapeDtypeStruct((rows, hg_cols), F32)],
        compiler_params=_cparams("parallel"),
        name="in_proj",
    )(x2d, g, w_stack, rope_t, qg, kg, gsum)


def _attn_kernel(lam_ref, bounded_ref, qt_ref, k_ref, vt_ref, og_ref, o_ref,
                 qh_ref, s_buf, p_buf, a_buf, acc_ref, m_ref, l_ref, *, out_scale):
    seq = k_ref.shape[1]
    tk = p_buf.shape[2]
    tk_gen = s_buf.shape[2]
    assert (seq // tk) % 2 == 0 and (seq // tk_gen) % 2 == 0
    qt = qt_ref[0]
    row = lax.broadcasted_iota(jnp.int32, (LANES, 1), 0)
    zero = jnp.zeros_like(qt)
    qh_ref[0] = jnp.where(row < DA_HEAD_DIM, qt, zero)
    qh_ref[1] = jnp.where(row >= DA_HEAD_DIM, qt, zero)
    acc_ref[...] = jnp.zeros(acc_ref.shape, F32)
    l_ref[...] = jnp.zeros(l_ref.shape, F32)

    def k_chunk(j, size):
        return k_ref[0, pl.ds(pl.multiple_of(j * size, size), size), :]

    def vt_chunk(j, size):
        return vt_ref[0, :, pl.ds(pl.multiple_of(j * size, size), size)]

    bounded = bounded_ref[0, 0] != 0

    @pl.when(bounded)
    def _():
        n_chunks = seq // tk

        def exp_scores(j, slot):
            kc = k_chunk(j, tk)
            for c in range(2):
                s = jnp.dot(kc, qh_ref[c], preferred_element_type=F32)
                p = jnp.exp(s)
                l_ref[c] += jnp.sum(p, axis=0, keepdims=True)
                p_buf[slot, c] = p.astype(BF16)

        def values(j, slot):
            vc = vt_chunk(j, tk)
            for c in range(2):
                acc_ref[c] += jnp.dot(vc, p_buf[slot, c], preferred_element_type=F32)

        exp_scores(0, 0)

        def body(g, carry):
            j = 2 * g
            exp_scores(j + 1, 1)
            values(j, 0)
            exp_scores(j + 2, 0)
            values(j + 1, 1)
            return carry

        lax.fori_loop(0, n_chunks // 2 - 1, body, 0)
        exp_scores(n_chunks - 1, 1)
        values(n_chunks - 2, 0)
        values(n_chunks - 1, 1)

    @pl.when(jnp.logical_not(bounded))
    def _():
        n_chunks = seq // tk_gen
        m_ref[...] = jnp.full(m_ref.shape, -jnp.inf, F32)

        def scores(j, slot):
            kc = k_chunk(j, tk_gen)
            for c in range(2):
                s_buf[slot, c] = jnp.dot(kc, qh_ref[c], preferred_element_type=F32)

        def softmax(slot):
            for c in range(2):
                s = s_buf[slot, c]
                m_prev = m_ref[c]
                m_new = jnp.maximum(m_prev, jnp.max(s, axis=0, keepdims=True))
                p = jnp.exp(s - m_new)
                alpha = jnp.exp(m_prev - m_new)
                l_ref[c] = alpha * l_ref[c] + jnp.sum(p, axis=0, keepdims=True)
                m_ref[c] = m_new
                p_buf[slot, c, :tk_gen, :] = p.astype(BF16)
                a_buf[slot, c] = alpha

        def values(j, slot):
            vc = vt_chunk(j, tk_gen)
            for c in range(2):
                acc_ref[c] = (a_buf[slot, c] * acc_ref[c]
                              + jnp.dot(vc, p_buf[slot, c, :tk_gen, :], preferred_element_type=F32))

        scores(0, 0)
        scores(1, 1)
        softmax(0)

        def body(g, carry):
            j = 2 * g
            scores(j, 0)
            softmax(1)
            values(j - 2, 0)
            scores(j + 1, 1)
            softmax(0)
            values(j - 1, 1)
            return carry

        lax.fori_loop(1, n_chunks // 2, body, 0)
        softmax(1)
        values(n_chunks - 2, 0)
        values(n_chunks - 1, 1)

    ot = acc_ref[0] / l_ref[0] - lam_ref[0, 0] * (acc_ref[1] / l_ref[1])
    o_ref[0] = (_rms_rows(ot.T, og_ref[...]) * out_scale).astype(BF16)


def _score_bound(q_gain, k_gain):
    return 1.01 * DA_HEAD_DIM ** 0.5 * jnp.max(jnp.abs(q_gain)) * jnp.max(jnp.abs(k_gain))


def _attention(qt, k, vt, lam, bounded, out_g, out_scale):
    batch, seq, _ = k.shape
    tq = min(ATTN_Q_COLS, seq)
    tk = min(ATTN_K_ROWS, seq // 2)
    tk_gen = min(ATTN_K_ROWS_GENERAL, seq // 2)
    kernel = functools.partial(_attn_kernel, out_scale=out_scale)
    return pl.pallas_call(
        kernel,
        grid=(batch, DA_HEADS, seq // tq),
        in_specs=[pl.BlockSpec(memory_space=pltpu.SMEM), pl.BlockSpec(memory_space=pltpu.SMEM),
                  pl.BlockSpec((1, LANES, tq), lambda b, h, i: (b, h, i)),
                  pl.BlockSpec((1, seq, LANES), lambda b, h, i: (b, 0, h)),
                  pl.BlockSpec((1, DA_V_DIM, seq), lambda b, h, i: (b, h, 0)),
                  pl.BlockSpec((1, DA_V_DIM), lambda b, h, i: (0, 0))],
        out_specs=pl.BlockSpec((1, tq, DA_V_DIM), lambda b, h, i: (b, i, h)),
        out_shape=jax.ShapeDtypeStruct((batch, seq, DA_WIDTH), BF16),
        scratch_shapes=[pltpu.VMEM((2, LANES, tq), BF16),
                        pltpu.VMEM((2, 2, tk_gen, tq), F32),
                        pltpu.VMEM((2, 2, tk, tq), BF16),
                        pltpu.VMEM((2, 2, 1, tq), F32),
                        pltpu.VMEM((2, DA_V_DIM, tq), F32),
                        pltpu.VMEM((2, 1, tq), F32),
                        pltpu.VMEM((2, 1, tq), F32)],
        compiler_params=_cparams("parallel", "parallel", "arbitrary"),
        name="attention",
    )(lam, bounded, qt, k, vt, out_g)


def _row(t, j):
    return jnp.broadcast_to(t[j:j + 1, :], t.shape)


def _pair_masks(chunk, reverse):
    rows = lax.broadcasted_iota(jnp.int32, (chunk, chunk), 0)
    cols = lax.broadcasted_iota(jnp.int32, (chunk, chunk), 1)
    masks = []
    blk = SUBLANES
    while blk < chunk:
        if reverse:
            masks.append(((rows // blk) + 1 == (cols // blk)) & ((cols // blk) % 2 == 1))
        else:
            masks.append(((rows // blk) == (cols // blk) + 1) & ((rows // blk) % 2 == 1))
        blk *= 2
    return masks


def _hgrn_prep(q, z, lb, reverse):
    chunk = q.shape[0]
    n_tiles = chunk // SUBLANES
    e = jnp.exp(-jnp.abs(z))
    one_plus_e = 1.0 + e
    log_sig = jnp.minimum(z, 0.0) - jnp.log(one_plus_e)
    log_f = jnp.minimum(log_sig + jnp.log(1.0 + lb * jnp.exp(jnp.minimum(-z, EXP_CLAMP))), 0.0)
    k = (1.0 - lb) * (jnp.where(z >= 0, e, 1.0) / one_plus_e)

    sub = lax.broadcasted_iota(jnp.int32, (SUBLANES, LANES), 0)
    tiles = lambda a: [a[i * SUBLANES:(i + 1) * SUBLANES, :] for i in range(n_tiles)]
    q_t, k_t, lf_t = tiles(q), tiles(k), tiles(log_f)
    c_t = []
    for lf in lf_t:
        c = lf
        for sh in (1, 2, 4):
            if reverse:
                c = c + jnp.where(sub < SUBLANES - sh, pltpu.roll(c, SUBLANES - sh, axis=0), 0.0)
            else:
                c = c + jnp.where(sub >= sh, pltpu.roll(c, sh, axis=0), 0.0)
        c_t.append(c)
    return k, q_t, k_t, c_t


def _hgrn_finish(q, v, k, q_t, k_t, c_t, st_ref, base_mask, pair_masks, reverse, bounded):
    chunk = q.shape[0]
    n_tiles = chunk // SUBLANES
    sub = lax.broadcasted_iota(jnp.int32, (SUBLANES, LANES), 0)
    edge = 0 if reverse else SUBLANES - 1
    order = (lambda i: n_tiles - 1 - i) if reverse else (lambda i: i)
    nt_dims = (((1,), (1,)), ((), ()))

    def extend(c_t, m):
        out = []
        for i in range(n_tiles):
            pos = order(i)
            if (pos // m) % 2 == 1:
                last = (pos // m) * m - 1
                out.append(c_t[i] + _row(c_t[order(last)], edge))
            else:
                out.append(c_t[i])
        return out

    m = 1
    level = 0
    if bounded:
        while m * SUBLANES < min(HG_BASE_ROWS, chunk):
            c_t = extend(c_t, m)
            m *= 2
            level += 1
        qm = jnp.concatenate([q_t[i] * jnp.exp(c_t[i]) for i in range(n_tiles)], axis=0).astype(BF16)
        km = jnp.concatenate([k_t[i] * jnp.exp(-c_t[i]) for i in range(n_tiles)], axis=0).astype(BF16)
        sc = lax.dot_general(qm, km, nt_dims, preferred_element_type=F32)
        a_mat = jnp.where(base_mask, sc, 0.0)
    else:
        lane = lax.broadcasted_iota(jnp.int32, (SUBLANES, LANES), 1)
        a_tiles = []
        for i in range(n_tiles):
            a = jnp.zeros((SUBLANES, LANES), F32)
            for j in range(SUBLANES):
                valid = (sub <= j) if reverse else (sub >= j)
                term = jnp.where(valid, jnp.exp(c_t[i] - _row(c_t[i], j)) * q_t[i] * _row(k_t[i], j), 0.0)
                col = jnp.sum(term, axis=1, keepdims=True)
                a = jnp.where(lane == i * SUBLANES + j, col, a)
            a_tiles.append(a)
        a_mat = jnp.concatenate(a_tiles, axis=0)
        if chunk < LANES:
            a_mat = a_mat[:, :chunk]

    while m < n_tiles:
        qs, ks = [], []
        for i in range(n_tiles):
            pos = order(i)
            is_query = (pos // m) % 2 == 1
            if is_query:
                qs.append(q_t[i] * jnp.exp(c_t[i]))
                ks.append(jnp.zeros((SUBLANES, LANES), F32))
            else:
                last = (pos // m) * m + m - 1
                tot = _row(c_t[order(last)], edge)
                qs.append(jnp.zeros((SUBLANES, LANES), F32))
                ks.append(k_t[i] * jnp.exp(tot - c_t[i]))
        qm = jnp.concatenate(qs, axis=0).astype(BF16)
        km = jnp.concatenate(ks, axis=0).astype(BF16)
        sc = lax.dot_general(qm, km, nt_dims, preferred_element_type=F32)
        a_mat = jnp.where(pair_masks[level], sc, a_mat)
        c_t = extend(c_t, m)
        m *= 2
        level += 1

    total = _row(c_t[order(n_tiles - 1)], edge)
    c_full = jnp.concatenate(c_t, axis=0)
    q_in = (q * jnp.exp(c_full)).astype(BF16)
    k_out = (k * jnp.exp(jnp.concatenate([total] * n_tiles, axis=0) - c_full)).astype(BF16)
    v16 = v.astype(BF16)
    st = st_ref[...]
    o = (jnp.dot(a_mat.astype(BF16), v16, preferred_element_type=F32)
         + lax.dot_general(q_in, st.astype(BF16), nt_dims, preferred_element_type=F32))
    decay = jnp.exp(total[:1, :])
    st_ref[...] = st * decay + lax.dot_general(v16, k_out, (((0,), (0,)), ((), ())),
                                               preferred_element_type=F32)
    return o


def _hgrn_kernel(qf_ref, zf_ref, vf_ref, qb_ref, zb_ref, vb_ref, lb_ref, of_ref, ob_ref, st_ref):
    @pl.when(pl.program_id(1) == 0)
    def _():
        st_ref[...] = jnp.zeros(st_ref.shape, F32)

    c = HG_CHUNK
    base = min(HG_BASE_ROWS, c)
    n_chunks = qf_ref.shape[1] // c
    rows = lax.broadcasted_iota(jnp.int32, (c, c), 0)
    cols = lax.broadcasted_iota(jnp.int32, (c, c), 1)
    same_base = (rows // base) == (cols // base)
    dirs = ((qf_ref, zf_ref, vf_ref, of_ref, same_base & (cols <= rows), _pair_masks(c, reverse=False)),
            (qb_ref, zb_ref, vb_ref, ob_ref, same_base & (cols >= rows), _pair_masks(c, reverse=True)))

    steps = HG_STEPS_PER_ITER
    assert n_chunks % steps == 0

    def body(g, carry):
        row_sels = []
        for s in range(steps):
            j = g * steps + s
            row_sels.append((pl.ds(pl.multiple_of(j * c, c), c),
                             pl.ds(pl.multiple_of((n_chunks - 1 - j) * c, c), c)))
        low = jnp.zeros((1, HG_WIDTH), F32)
        for row_sel in row_sels:
            for d in range(2):
                zneg = jnp.minimum(dirs[d][1][0, row_sel[d], :], 0.0)
                for blk in range(c // base):
                    low = jnp.minimum(low, jnp.sum(zneg[blk * base:(blk + 1) * base, :], axis=0, keepdims=True))
        bounded = jnp.min(low) - base * math.log(2.0) >= -HG_BASE_BOUND

        def chunk_steps(is_bounded):
            for row_sel in row_sels:
                for hd in range(HG_HEADS):
                    cs = slice(hd * HG_DIM, (hd + 1) * HG_DIM)
                    for d, (q_ref, z_ref, v_ref, o_ref, base_mask, pair_masks) in enumerate(dirs):
                        q = q_ref[0, row_sel[d], cs]
                        k, q_t, k_t, c_t = _hgrn_prep(q, z_ref[0, row_sel[d], cs], lb_ref[d:d + 1, cs],
                                                      reverse=bool(d))
                        o_ref[0, row_sel[d], cs] = _hgrn_finish(
                            q, v_ref[0, row_sel[d], cs], k, q_t, k_t, c_t, st_ref.at[d, hd],
                            base_mask, pair_masks, reverse=bool(d), bounded=is_bounded)

        pl.when(bounded)(functools.partial(chunk_steps, True))
        pl.when(jnp.logical_not(bounded))(functools.partial(chunk_steps, False))
        return carry

    lax.fori_loop(0, n_chunks // steps, body, 0)


def _hgrn(hg, lb):
    batch, seq, _ = hg.shape
    c = min(HG_BLOCK, seq)
    n = seq // c
    fwd = lambda col: pl.BlockSpec((1, c, HG_WIDTH), lambda b, j: (b, j, col))
    bwd = lambda col: pl.BlockSpec((1, c, HG_WIDTH), lambda b, j: (b, n - 1 - j, col))
    return pl.pallas_call(
        _hgrn_kernel,
        grid=(batch, n),
        in_specs=[fwd(0), fwd(1), fwd(3), bwd(0), bwd(2), bwd(3),
                  pl.BlockSpec((2, HG_WIDTH), lambda b, j: (0, 0))],
        out_specs=[fwd(0), bwd(0)],
        out_shape=[jax.ShapeDtypeStruct((batch, seq, HG_WIDTH), F32)] * 2,
        scratch_shapes=[pltpu.VMEM((2, HG_HEADS, HG_DIM, HG_DIM), F32)],
        compiler_params=_cparams("parallel", "arbitrary"),
        name="hgrn",
    )(hg, hg, hg, hg, hg, hg, lb)


def _out_proj_kernel(ya_ref, of_ref, ob_ref, gate_ref, x_ref, w_ref, hgg_ref, x_out):
    o = of_ref[...] + ob_ref[...]
    gate = gate_ref[...]
    parts = []
    for hd in range(HG_HEADS):
        cs = slice(hd * HG_DIM, (hd + 1) * HG_DIM)
        g = gate[:, cs]
        parts.append((_rms_rows(o[:, cs], hgg_ref[...]) * (g * jax.nn.sigmoid(g))).astype(BF16))
    y = jnp.concatenate([ya_ref[...]] + parts, axis=1)
    x_out[...] = x_ref[...] + jnp.dot(y, w_ref[...], preferred_element_type=F32)


def _out_proj(ya, o_f, o_b, hg, x2d, w_stack, layer, hg_out_g):
    rows, d_model = x2d.shape
    tm = PROJ_ROWS
    gate_col = hg.shape[1] // HG_WIDTH - 1
    row_blk = lambda cols: pl.BlockSpec((tm, cols), lambda i: (i, 0))
    return pl.pallas_call(
        _out_proj_kernel,
        grid=(rows // tm,),
        in_specs=[row_blk(DA_WIDTH), row_blk(HG_WIDTH), row_blk(HG_WIDTH),
                  pl.BlockSpec((tm, HG_WIDTH), lambda i: (i, gate_col)),
                  row_blk(d_model), _layer_resident(w_stack, layer), _resident((1, HG_DIM))],
        out_specs=row_blk(d_model),
        out_shape=jax.ShapeDtypeStruct((rows, d_model), F32),
        compiler_params=_cparams("parallel"),
        name="out_proj",
    )(ya, o_f, o_b, hg, x2d, w_stack, hg_out_g)


def _ffn_kernel(x_ref, xp_ref, xn_ref, fg_ref, wup_ref, cw_ref, cb_ref, wdn_ref, o_ref,
                hext_ref, u_buf, g_ref, *, tiles_per_seq):
    tm = x_ref.shape[0]
    i = pl.program_id(0)
    first = i % tiles_per_seq == 0
    last = i % tiles_per_seq == tiles_per_seq - 1
    fg = fg_ref[...]
    pad = jnp.zeros((HALO - SUBLANES, x_ref.shape[1]), F32)
    hp = jnp.where(first, 0.0, _rms_rows(xp_ref[...], fg))
    hn = jnp.where(last, 0.0, _rms_rows(xn_ref[...], fg))
    hext_ref[:HALO] = jnp.concatenate([pad, hp], axis=0).astype(BF16)
    hext_ref[HALO:HALO + tm] = _rms_rows(x_ref[...], fg).astype(BF16)
    hext_ref[HALO + tm:] = jnp.concatenate([hn, pad], axis=0).astype(BF16)
    n_chunks = D_FF // FFN_COLS

    def cols(c, part):
        return slice(part * D_FF + c * FFN_COLS, part * D_FF + (c + 1) * FFN_COLS)

    def up(c, slot):
        for part in range(2):
            u_buf[slot, part] = jnp.dot(hext_ref[...], wup_ref[:, cols(c, part)],
                                        preferred_element_type=F32)

    def conv_gate(c, slot):
        outs = []
        for part in range(2):
            w = cw_ref[:, cols(c, part)]
            u = u_buf[slot, part]
            rows = u.shape[0]
            conv = (w[0:1] * pltpu.roll(u, 1, axis=0) + w[1:2] * u
                    + w[2:3] * pltpu.roll(u, rows - 1, axis=0))
            outs.append(conv[HALO:HALO + tm] + cb_ref[:, cols(c, part)])
        a, v = outs
        g_ref[:, c * FFN_COLS:(c + 1) * FFN_COLS] = (a * jax.nn.sigmoid(a) * v).astype(BF16)

    n_slots = u_buf.shape[0]
    up(0, 0)
    for c in range(n_chunks):
        if c + 1 < n_chunks:
            up(c + 1, (c + 1) % n_slots)
        conv_gate(c, c % n_slots)
    o_ref[...] = x_ref[...] + jnp.dot(g_ref[...], wdn_ref[...], preferred_element_type=F32)


def _ffn(x1, ffn_g, w_up_stack, conv_w, conv_b, w_down_stack, layer, seq):
    rows, d_model = x1.shape
    tm = PROJ_ROWS
    tiles_per_seq = seq // tm
    per_tile = tm // SUBLANES
    n_halo_blocks = rows // SUBLANES
    kernel = functools.partial(_ffn_kernel, tiles_per_seq=tiles_per_seq)
    return pl.pallas_call(
        kernel,
        grid=(rows // tm,),
        in_specs=[pl.BlockSpec((tm, d_model), lambda i: (i, 0)),
                  pl.BlockSpec((SUBLANES, d_model), lambda i: (jnp.maximum(i * per_tile - 1, 0), 0)),
                  pl.BlockSpec((SUBLANES, d_model),
                               lambda i: (jnp.minimum((i + 1) * per_tile, n_halo_blocks - 1), 0)),
                  _resident((1, d_model)),
                  _layer_resident(w_up_stack, layer), _resident(conv_w.shape), _resident(conv_b.shape),
                  _layer_resident(w_down_stack, layer)],
        out_specs=pl.BlockSpec((tm, d_model), lambda i: (i, 0)),
        out_shape=jax.ShapeDtypeStruct((rows, d_model), F32),
        scratch_shapes=[pltpu.VMEM((tm + 2 * HALO, d_model), BF16),
                        pltpu.VMEM((2, 2, tm + 2 * HALO, FFN_COLS), F32),
                        pltpu.VMEM((tm, D_FF), BF16)],
        compiler_params=_cparams("parallel"),
        name="ffn",
    )(x1, x1, x1, ffn_g, w_up_stack, conv_w, conv_b, w_down_stack)


def _rope_tables(positions):
    half = DA_HEAD_DIM // 2
    inv_freq = ROPE_THETA ** (-jnp.arange(half, dtype=F32) / half)
    ang = positions.astype(F32)[:, :, None] * inv_freq
    cos, sin = lax.optimization_barrier((jnp.cos(ang), jnp.sin(ang)))
    assert LANES == 4 * half
    return jnp.concatenate([cos, cos, -sin, sin], axis=-1).reshape(-1, LANES)


def kernel(x, positions, mix_norm_g, w_in, q_norm_g, k_norm_g, lam_q1, lam_k1, lam_q2, lam_k2,
           diff_out_g, hg_lb_logits, hg_out_g, w_out, ffn_norm_g, w_up, conv_w, conv_b, w_down):
    batch, seq, d_model = x.shape
    assert seq % PROJ_ROWS == 0 and seq % ATTN_Q_COLS == 0 and seq % HG_BLOCK == 0
    rope_t = _rope_tables(positions)
    p = jax.nn.softmax(hg_lb_logits.astype(F32), axis=0)
    lower_bounds = jnp.clip(jnp.cumsum(p, axis=0) - p[0:1], 0.0, 1.0 - 1e-4)
    gid = np.arange(DA_QK) // DA_HEAD_DIM
    gsum = jnp.asarray(gid[:, None] == gid[None, :], BF16)
    n_groups = DA_QK // DA_HEAD_DIM

    w_in16, w_out16, w_up16, w_down16 = (w.astype(BF16) for w in (w_in, w_out, w_up, w_down))

    x2d = x.reshape(batch * seq, d_model)
    for l in range(DEPTH):
        lam_init = 0.8 - 0.6 * float(np.exp(-0.3 * l))
        lam = (jnp.exp(jnp.sum(lam_q1[l].astype(F32) * lam_k1[l].astype(F32)))
               - jnp.exp(jnp.sum(lam_q2[l].astype(F32) * lam_k2[l].astype(F32))) + lam_init)
        qt, k, vt, hg = _in_proj(
            x2d, mix_norm_g[l][None, :], w_in16, l, rope_t,
            jnp.tile(q_norm_g[l], n_groups)[None, :], jnp.tile(k_norm_g[l], n_groups)[None, :],
            gsum, batch, seq)
        bounded = (_score_bound(q_norm_g[l], k_norm_g[l]) <= SCORE_BOUND).astype(jnp.int32)
        y_a = _attention(qt, k.reshape(batch, seq, DA_QK), vt, lam.reshape(1, 1), bounded.reshape(1, 1),
                         diff_out_g[l][None, :], 1.0 - lam_init)
        o_f, o_b = _hgrn(hg.reshape(batch, seq, -1), lower_bounds[l])
        x1 = _out_proj(y_a.reshape(batch * seq, DA_WIDTH), o_f.reshape(batch * seq, HG_WIDTH),
                       o_b.reshape(batch * seq, HG_WIDTH), hg, x2d, w_out16, l,
                       hg_out_g[l][None, :])
        x2d = _ffn(x1, ffn_norm_g[l][None, :], w_up16, conv_w[l], conv_b[l][None, :],
                   w_down16, l, seq)
    return x2d.reshape(batch, seq, d_model)
```

```python
import functools
import math

import jax
import jax.numpy as jnp
import numpy as np
from jax import lax
from jax.experimental import pallas as pl
from jax.experimental.pallas import tpu as pltpu

F32 = jnp.float32
BF16 = jnp.bfloat16

DEPTH = 2
DA_HEADS = 4
DA_HEAD_DIM = 64
DA_V_DIM = 128
DA_QK = 512
DA_WIDTH = 512
HG_HEADS = 4
HG_DIM = 128
HG_WIDTH = 512
D_FF = 2816
ROPE_THETA = 10000.0
EPS = 1e-6
EXP_CLAMP = 30.0

LANES = 128
SUBLANES = 8
VMEM_LIMIT_BYTES = 56 * 1024 * 1024

PROJ_ROWS = 512
ATTN_Q_COLS = 2048
ATTN_K_ROWS = 512
ATTN_K_ROWS_GENERAL = 256
SCORE_BOUND = 20.0
HG_CHUNK = 64
HG_BLOCK = 512
HG_STEPS_PER_ITER = 4
HG_BASE_ROWS = 32
HG_BASE_BOUND = 64.0
FFN_COLS = 256
HALO = 2 * SUBLANES


def _cparams(*sem):
    return pltpu.CompilerParams(dimension_semantics=sem, vmem_limit_bytes=VMEM_LIMIT_BYTES)


def _resident(shape):
    nd = len(shape)
    return pl.BlockSpec(shape, lambda *_: (0,) * nd, pipeline_mode=pl.Buffered(1))


def _layer_resident(stacked, layer):
    _, rows, cols = stacked.shape
    return pl.BlockSpec((None, rows, cols), lambda *_: (layer, 0, 0), pipeline_mode=pl.Buffered(1))


def _rms_rows(x, g):
    ms = jnp.mean(x * x, axis=-1, keepdims=True)
    return x * lax.rsqrt(ms + EPS) * g


def _group_mean_sq(t, gsum):
    ss = jnp.dot((t * t).astype(BF16), gsum, preferred_element_type=F32)
    return ss * (1.0 / DA_HEAD_DIM)


def _rope(t, cos, sin_signed, first_half):
    width = t.shape[-1]
    half = DA_HEAD_DIM // 2
    swapped = jnp.where(first_half, pltpu.roll(t, width - half, axis=1), pltpu.roll(t, half, axis=1))
    return t * cos + swapped * sin_signed


def _in_proj_kernel(x_ref, g_ref, w_ref, cs_ref, qg_ref, kg_ref, gsum_ref,
                    qt_out, k_out, vt_out, hg_out):
    h = _rms_rows(x_ref[...], g_ref[...]).astype(BF16)

    def proj(lo, hi):
        return jnp.dot(h, w_ref[:, lo:hi], preferred_element_type=F32)

    cs = cs_ref[...]
    sc = pltpu.roll(cs, DA_HEAD_DIM, axis=1)
    low_half = lax.broadcasted_iota(jnp.int32, (1, LANES), 1) < DA_HEAD_DIM
    cos = jnp.concatenate([jnp.where(low_half, cs, sc)] * (DA_QK // LANES), axis=1)
    sin = jnp.concatenate([jnp.where(low_half, sc, cs)] * (DA_QK // LANES), axis=1)
    lane = lax.broadcasted_iota(jnp.int32, (1, DA_QK), 1)
    first_half = (lane % DA_HEAD_DIM) < (DA_HEAD_DIM // 2)
    gsum = gsum_ref[...]

    q = proj(0, DA_QK)
    q = q * lax.rsqrt(_group_mean_sq(q, gsum) + EPS) * qg_ref[...]
    q = _rope(q, cos, sin, first_half) * (DA_HEAD_DIM ** -0.5)
    qt_out[0] = q.T.astype(BF16)

    k = proj(DA_QK, 2 * DA_QK)
    k = k * lax.rsqrt(_group_mean_sq(k, gsum) + EPS) * kg_ref[...]
    k_out[...] = _rope(k, cos, sin, first_half).astype(BF16)

    vt_out[0] = proj(2 * DA_QK, 2 * DA_QK + DA_WIDTH).T.astype(BF16)

    hg_out[...] = proj(2 * DA_QK + DA_WIDTH, w_ref.shape[1])


def _in_proj(x2d, g, w_stack, layer, rope_t, qg, kg, gsum, batch, seq):
    rows, d_model = x2d.shape
    d_in = w_stack.shape[2]
    tm = PROJ_ROWS
    tiles_per_seq = seq // tm
    hg_cols = d_in - 2 * DA_QK - DA_WIDTH
    row_blk = lambda cols: pl.BlockSpec((tm, cols), lambda i: (i, 0))
    t_blk = lambda ch: pl.BlockSpec((1, ch, tm), lambda i: (i // tiles_per_seq, 0, i % tiles_per_seq))
    return pl.pallas_call(
        _in_proj_kernel,
        grid=(rows // tm,),
        in_specs=[row_blk(d_model), _resident((1, d_model)), _layer_resident(w_stack, layer),
                  row_blk(LANES), _resident((1, DA_QK)), _resident((1, DA_QK)),
                  _resident((DA_QK, DA_QK))],
        out_specs=[t_blk(DA_QK), row_blk(DA_QK), t_blk(DA_WIDTH), row_blk(hg_cols)],
        out_shape=[jax.ShapeDtypeStruct((batch, DA_QK, seq), BF16),
                   jax.ShapeDtypeStruct((rows, DA_QK), BF16),
                   jax.ShapeDtypeStruct((batch, DA_WIDTH, seq), BF16),
                   jax.ShapeDtypeStruct((rows, hg_cols), F32)],
        compiler_params=_cparams("parallel"),
        name="in_proj",
    )(x2d, g, w_stack, rope_t, qg, kg, gsum)


def _attn_kernel(lam_ref, bounded_ref, qt_ref, k_ref, vt_ref, og_ref, o_ref,
                 qh_ref, s_buf, p_buf, a_buf, acc_ref, m_ref, l_ref, *, out_scale):
    seq = k_ref.shape[1]
    tk = p_buf.shape[2]
    tk_gen = s_buf.shape[2]
    assert (seq // tk) % 2 == 0 and (seq // tk_gen) % 2 == 0
    qt = qt_ref[0]
    row = lax.broadcasted_iota(jnp.int32, (LANES, 1), 0)
    zero = jnp.zeros_like(qt)
    qh_ref[0] = jnp.where(row < DA_HEAD_DIM, qt, zero)
    qh_ref[1] = jnp.where(row >= DA_HEAD_DIM, qt, zero)
    acc_ref[...] = jnp.zeros(acc_ref.shape, F32)
    l_ref[...] = jnp.zeros(l_ref.shape, F32)

    def k_chunk(j, size):
        return k_ref[0, pl.ds(pl.multiple_of(j * size, size), size), :]

    def vt_chunk(j, size):
        return vt_ref[0, :, pl.ds(pl.multiple_of(j * size, size), size)]

    bounded = bounded_ref[0, 0] != 0

    @pl.when(bounded)
    def _():
        n_chunks = seq // tk

        def exp_scores(j, slot):
            kc = k_chunk(j, tk)
            for c in range(2):
                s = jnp.dot(kc, qh_ref[c], preferred_element_type=F32)
                p = jnp.exp(s)
                l_ref[c] += jnp.sum(p, axis=0, keepdims=True)
                p_buf[slot, c] = p.astype(BF16)

        def values(j, slot):
            vc = vt_chunk(j, tk)
            for c in range(2):
                acc_ref[c] += jnp.dot(vc, p_buf[slot, c], preferred_element_type=F32)

        exp_scores(0, 0)

        def body(g, carry):
            j = 2 * g
            exp_scores(j + 1, 1)
            values(j, 0)
            exp_scores(j + 2, 0)
            values(j + 1, 1)
            return carry

        lax.fori_loop(0, n_chunks // 2 - 1, body, 0)
        exp_scores(n_chunks - 1, 1)
        values(n_chunks - 2, 0)
        values(n_chunks - 1, 1)

    @pl.when(jnp.logical_not(bounded))
    def _():
        n_chunks = seq // tk_gen
        m_ref[...] = jnp.full(m_ref.shape, -jnp.inf, F32)

        def scores(j, slot):
            kc = k_chunk(j, tk_gen)
            for c in range(2):
                s_buf[slot, c] = jnp.dot(kc, qh_ref[c], preferred_element_type=F32)

        def softmax(slot):
            for c in range(2):
                s = s_buf[slot, c]
                m_prev = m_ref[c]
                m_new = jnp.maximum(m_prev, jnp.max(s, axis=0, keepdims=True))
                p = jnp.exp(s - m_new)
                alpha = jnp.exp(m_prev - m_new)
                l_ref[c] = alpha * l_ref[c] + jnp.sum(p, axis=0, keepdims=True)
                m_ref[c] = m_new
                p_buf[slot, c, :tk_gen, :] = p.astype(BF16)
                a_buf[slot, c] = alpha

        def values(j, slot):
            vc = vt_chunk(j, tk_gen)
            for c in range(2):
                acc_ref[c] = (a_buf[slot, c] * acc_ref[c]
                              + jnp.dot(vc, p_buf[slot, c, :tk_gen, :], preferred_element_type=F32))

        scores(0, 0)
        scores(1, 1)
        softmax(0)

        def body(g, carry):
            j = 2 * g
            scores(j, 0)
            softmax(1)
            values(j - 2, 0)
            scores(j + 1, 1)
            softmax(0)
            values(j - 1, 1)
            return carry

        lax.fori_loop(1, n_chunks // 2, body, 0)
        softmax(1)
        values(n_chunks - 2, 0)
        values(n_chunks - 1, 1)

    ot = acc_ref[0] / l_ref[0] - lam_ref[0, 0] * (acc_ref[1] / l_ref[1])
    o_ref[0] = (_rms_rows(ot.T, og_ref[...]) * out_scale).astype(BF16)


def _score_bound(q_gain, k_gain):
    return 1.01 * DA_HEAD_DIM ** 0.5 * jnp.max(jnp.abs(q_gain)) * jnp.max(jnp.abs(k_gain))


def _attention(qt, k, vt, lam, bounded, out_g, out_scale):
    batch, seq, _ = k.shape
    tq = min(ATTN_Q_COLS, seq)
    tk = min(ATTN_K_ROWS, seq // 2)
    tk_gen = min(ATTN_K_ROWS_GENERAL, seq // 2)
    kernel = functools.partial(_attn_kernel, out_scale=out_scale)
    return pl.pallas_call(
        kernel,
        grid=(batch, DA_HEADS, seq // tq),
        in_specs=[pl.BlockSpec(memory_space=pltpu.SMEM), pl.BlockSpec(memory_space=pltpu.SMEM),
                  pl.BlockSpec((1, LANES, tq), lambda b, h, i: (b, h, i)),
                  pl.BlockSpec((1, seq, LANES), lambda b, h, i: (b, 0, h)),
                  pl.BlockSpec((1, DA_V_DIM, seq), lambda b, h, i: (b, h, 0)),
                  pl.BlockSpec((1, DA_V_DIM), lambda b, h, i: (0, 0))],
        out_specs=pl.BlockSpec((1, tq, DA_V_DIM), lambda b, h, i: (b, i, h)),
        out_shape=jax.ShapeDtypeStruct((batch, seq, DA_WIDTH), BF16),
        scratch_shapes=[pltpu.VMEM((2, LANES, tq), BF16),
                        pltpu.VMEM((2, 2, tk_gen, tq), F32),
                        pltpu.VMEM((2, 2, tk, tq), BF16),
                        pltpu.VMEM((2, 2, 1, tq), F32),
                        pltpu.VMEM((2, DA_V_DIM, tq), F32),
                        pltpu.VMEM((2, 1, tq), F32),
                        pltpu.VMEM((2, 1, tq), F32)],
        compiler_params=_cparams("parallel", "parallel", "arbitrary"),
        name="attention",
    )(lam, bounded, qt, k, vt, out_g)


def _row(t, j):
    return jnp.broadcast_to(t[j:j + 1, :], t.shape)


def _pair_masks(chunk, reverse):
    rows = lax.broadcasted_iota(jnp.int32, (chunk, chunk), 0)
    cols = lax.broadcasted_iota(jnp.int32, (chunk, chunk), 1)
    masks = []
    blk = SUBLANES
    while blk < chunk:
        if reverse:
            masks.append(((rows // blk) + 1 == (cols // blk)) & ((cols // blk) % 2 == 1))
        else:
            masks.append(((rows // blk) == (cols // blk) + 1) & ((rows // blk) % 2 == 1))
        blk *= 2
    return masks


def _hgrn_prep(q, z, lb, reverse):
    chunk = q.shape[0]
    n_tiles = chunk // SUBLANES
    e = jnp.exp(-jnp.abs(z))
    one_plus_e = 1.0 + e
    log_sig = jnp.minimum(z, 0.0) - jnp.log(one_plus_e)
    log_f = jnp.minimum(log_sig + jnp.log(1.0 + lb * jnp.exp(jnp.minimum(-z, EXP_CLAMP))), 0.0)
    k = (1.0 - lb) * (jnp.where(z >= 0, e, 1.0) / one_plus_e)

    sub = lax.broadcasted_iota(jnp.int32, (SUBLANES, LANES), 0)
    tiles = lambda a: [a[i * SUBLANES:(i + 1) * SUBLANES, :] for i in range(n_tiles)]
    q_t, k_t, lf_t = tiles(q), tiles(k), tiles(log_f)
    c_t = []
    for lf in lf_t:
        c = lf
        for sh in (1, 2, 4):
            if reverse:
                c = c + jnp.where(sub < SUBLANES - sh, pltpu.roll(c, SUBLANES - sh, axis=0), 0.0)
            else:
                c = c + jnp.where(sub >= sh, pltpu.roll(c, sh, axis=0), 0.0)
        c_t.append(c)
    return k, q_t, k_t, c_t


def _hgrn_finish(q, v, k, q_t, k_t, c_t, st_ref, base_mask, pair_masks, reverse, bounded):
    chunk = q.shape[0]
    n_tiles = chunk // SUBLANES
    sub = lax.broadcasted_iota(jnp.int32, (SUBLANES, LANES), 0)
    edge = 0 if reverse else SUBLANES - 1
    order = (lambda i: n_tiles - 1 - i) if reverse else (lambda i: i)
    nt_dims = (((1,), (1,)), ((), ()))

    def extend(c_t, m):
        out = []
        for i in range(n_tiles):
            pos = order(i)
            if (pos // m) % 2 == 1:
                last = (pos // m) * m - 1
                out.append(c_t[i] + _row(c_t[order(last)], edge))
            else:
                out.append(c_t[i])
        return out

    m = 1
    level = 0
    if bounded:
        while m * SUBLANES < min(HG_BASE_ROWS, chunk):
            c_t = extend(c_t, m)
            m *= 2
            level += 1
        qm = jnp.concatenate([q_t[i] * jnp.exp(c_t[i]) for i in range(n_tiles)], axis=0).astype(BF16)
        km = jnp.concatenate([k_t[i] * jnp.exp(-c_t[i]) for i in range(n_tiles)], axis=0).astype(BF16)
        sc = lax.dot_general(qm, km, nt_dims, preferred_element_type=F32)
        a_mat = jnp.where(base_mask, sc, 0.0)
    else:
        lane = lax.broadcasted_iota(jnp.int32, (SUBLANES, LANES), 1)
        a_tiles = []
        for i in range(n_tiles):
            a = jnp.zeros((SUBLANES, LANES), F32)
            for j in range(SUBLANES):
                valid = (sub <= j) if reverse else (sub >= j)
                term = jnp.where(valid, jnp.exp(c_t[i] - _row(c_t[i], j)) * q_t[i] * _row(k_t[i], j), 0.0)
                col = jnp.sum(term, axis=1, keepdims=True)
                a = jnp.where(lane == i * SUBLANES + j, col, a)
            a_tiles.append(a)
        a_mat = jnp.concatenate(a_tiles, axis=0)
        if chunk < LANES:
            a_mat = a_mat[:, :chunk]

    while m < n_tiles:
        qs, ks = [], []
        for i in range(n_tiles):
            pos = order(i)
            is_query = (pos // m) % 2 == 1
            if is_query:
                qs.append(q_t[i] * jnp.exp(c_t[i]))
                ks.append(jnp.zeros((SUBLANES, LANES), F32))
            else:
                last = (pos // m) * m + m - 1
                tot = _row(c_t[order(last)], edge)
                qs.append(jnp.zeros((SUBLANES, LANES), F32))
                ks.append(k_t[i] * jnp.exp(tot - c_t[i]))
        qm = jnp.concatenate(qs, axis=0).astype(BF16)
        km = jnp.concatenate(ks, axis=0).astype(BF16)
        sc = lax.dot_general(qm, km, nt_dims, preferred_element_type=F32)
        a_mat = jnp.where(pair_masks[level], sc, a_mat)
        c_t = extend(c_t, m)
        m *= 2
        level += 1

    total = _row(c_t[order(n_tiles - 1)], edge)
    c_full = jnp.concatenate(c_t, axis=0)
    q_in = (q * jnp.exp(c_full)).astype(BF16)
    k_out = (k * jnp.exp(jnp.concatenate([total] * n_tiles, axis=0) - c_full)).astype(BF16)
    v16 = v.astype(BF16)
    st = st_ref[...]
    o = (jnp.dot(a_mat.astype(BF16), v16, preferred_element_type=F32)
         + lax.dot_general(q_in, st.astype(BF16), nt_dims, preferred_element_type=F32))
    decay = jnp.exp(total[:1, :])
    st_ref[...] = st * decay + lax.dot_general(v16, k_out, (((0,), (0,)), ((), ())),
                                               preferred_element_type=F32)
    return o


def _hgrn_kernel(qf_ref, zf_ref, vf_ref, qb_ref, zb_ref, vb_ref, lb_ref, of_ref, ob_ref, st_ref):
    @pl.when(pl.program_id(1) == 0)
    def _():
        st_ref[...] = jnp.zeros(st_ref.shape, F32)

    c = HG_CHUNK
    base = min(HG_BASE_ROWS, c)
    n_chunks = qf_ref.shape[1] // c
    rows = lax.broadcasted_iota(jnp.int32, (c, c), 0)
    cols = lax.broadcasted_iota(jnp.int32, (c, c), 1)
    same_base = (rows // base) == (cols // base)
    dirs = ((qf_ref, zf_ref, vf_ref, of_ref, same_base & (cols <= rows), _pair_masks(c, reverse=False)),
            (qb_ref, zb_ref, vb_ref, ob_ref, same_base & (cols >= rows), _pair_masks(c, reverse=True)))

    steps = HG_STEPS_PER_ITER
    assert n_chunks % steps == 0

    def body(g, carry):
        row_sels = []
        for s in range(steps):
            j = g * steps + s
            row_sels.append((pl.ds(pl.multiple_of(j * c, c), c),
                             pl.ds(pl.multiple_of((n_chunks - 1 - j) * c, c), c)))
        low = jnp.zeros((1, HG_WIDTH), F32)
        for row_sel in row_sels:
            for d in range(2):
                zneg = jnp.minimum(dirs[d][1][0, row_sel[d], :], 0.0)
                for blk in range(c // base):
                    low = jnp.minimum(low, jnp.sum(zneg[blk * base:(blk + 1) * base, :], axis=0, keepdims=True))
        bounded = jnp.min(low) - base * math.log(2.0) >= -HG_BASE_BOUND

        def chunk_steps(is_bounded):
            for row_sel in row_sels:
                for hd in range(HG_HEADS):
                    cs = slice(hd * HG_DIM, (hd + 1) * HG_DIM)
                    for d, (q_ref, z_ref, v_ref, o_ref, base_mask, pair_masks) in enumerate(dirs):
                        q = q_ref[0, row_sel[d], cs]
                        k, q_t, k_t, c_t = _hgrn_prep(q, z_ref[0, row_sel[d], cs], lb_ref[d:d + 1, cs],
                                                      reverse=bool(d))
                        o_ref[0, row_sel[d], cs] = _hgrn_finish(
                            q, v_ref[0, row_sel[d], cs], k, q_t, k_t, c_t, st_ref.at[d, hd],
                            base_mask, pair_masks, reverse=bool(d), bounded=is_bounded)

        pl.when(bounded)(functools.partial(chunk_steps, True))
        pl.when(jnp.logical_not(bounded))(functools.partial(chunk_steps, False))
        return carry

    lax.fori_loop(0, n_chunks // steps, body, 0)


def _hgrn(hg, lb):
    batch, seq, _ = hg.shape
    c = min(HG_BLOCK, seq)
    n = seq // c
    fwd = lambda col: pl.BlockSpec((1, c, HG_WIDTH), lambda b, j: (b, j, col))
    bwd = lambda col: pl.BlockSpec((1, c, HG_WIDTH), lambda b, j: (b, n - 1 - j, col))
    return pl.pallas_call(
        _hgrn_kernel,
        grid=(batch, n),
        in_specs=[fwd(0), fwd(1), fwd(3), bwd(0), bwd(2), bwd(3),
                  pl.BlockSpec((2, HG_WIDTH), lambda b, j: (0, 0))],
        out_specs=[fwd(0), bwd(0)],
        out_shape=[jax.ShapeDtypeStruct((batch, seq, HG_WIDTH), F32)] * 2,
        scratch_shapes=[pltpu.VMEM((2, HG_HEADS, HG_DIM, HG_DIM), F32)],
        compiler_params=_cparams("parallel", "arbitrary"),
        name="hgrn",
    )(hg, hg, hg, hg, hg, hg, lb)


def _out_proj_kernel(ya_ref, of_ref, ob_ref, gate_ref, x_ref, w_ref, hgg_ref, x_out):
    o = of_ref[...] + ob_ref[...]
    gate = gate_ref[...]
    parts = []
    for hd in range(HG_HEADS):
        cs = slice(hd * HG_DIM, (hd + 1) * HG_DIM)
        g = gate[:, cs]
        parts.append((_rms_rows(o[:, cs], hgg_ref[...]) * (g * jax.nn.sigmoid(g))).astype(BF16))
    y = jnp.concatenate([ya_ref[...]] + parts, axis=1)
    x_out[...] = x_ref[...] + jnp.dot(y, w_ref[...], preferred_element_type=F32)


def _out_proj(ya, o_f, o_b, hg, x2d, w_stack, layer, hg_out_g):
    rows, d_model = x2d.shape
    tm = PROJ_ROWS
    gate_col = hg.shape[1] // HG_WIDTH - 1
    deep = pl.Buffered(3)
    row_blk = lambda cols: pl.BlockSpec((tm, cols), lambda i: (i, 0), pipeline_mode=deep)

    def outer(ya_hbm, of_hbm, ob_hbm, hg_hbm, x_hbm, w_ref, hgg_ref, out_hbm):
        def tile(ya_ref, of_ref, ob_ref, gate_ref, x_ref, x_out):
            _out_proj_kernel(ya_ref, of_ref, ob_ref, gate_ref, x_ref, w_ref.at[layer], hgg_ref, x_out)

        pltpu.emit_pipeline(
            tile,
            grid=(rows // tm,),
            in_specs=[row_blk(DA_WIDTH), row_blk(HG_WIDTH), row_blk(HG_WIDTH),
                      pl.BlockSpec((tm, HG_WIDTH), lambda i: (i, gate_col), pipeline_mode=deep),
                      row_blk(d_model)],
            out_specs=[pl.BlockSpec((tm, d_model), lambda i: (i, 0))],
        )(ya_hbm, of_hbm, ob_hbm, hg_hbm, x_hbm, out_hbm)

    streamed = pl.BlockSpec(memory_space=pl.ANY)
    whole = pl.BlockSpec(memory_space=pltpu.VMEM)
    return pl.pallas_call(
        outer,
        in_specs=[streamed] * 5 + [whole, whole],
        out_specs=streamed,
        out_shape=jax.ShapeDtypeStruct((rows, d_model), F32),
        compiler_params=pltpu.CompilerParams(vmem_limit_bytes=VMEM_LIMIT_BYTES),
        name="out_proj",
    )(ya, o_f, o_b, hg, x2d, w_stack, hg_out_g)


def _ffn_kernel(x_ref, xp_ref, xn_ref, fg_ref, wup_ref, cw_ref, cb_ref, wdn_ref, o_ref,
                hext_ref, u_buf, g_ref, *, tiles_per_seq):
    tm = x_ref.shape[0]
    i = pl.program_id(0)
    first = i % tiles_per_seq == 0
    last = i % tiles_per_seq == tiles_per_seq - 1
    fg = fg_ref[...]
    pad = jnp.zeros((HALO - SUBLANES, x_ref.shape[1]), F32)
    hp = jnp.where(first, 0.0, _rms_rows(xp_ref[...], fg))
    hn = jnp.where(last, 0.0, _rms_rows(xn_ref[...], fg))
    hext_ref[:HALO] = jnp.concatenate([pad, hp], axis=0).astype(BF16)
    hext_ref[HALO:HALO + tm] = _rms_rows(x_ref[...], fg).astype(BF16)
    hext_ref[HALO + tm:] = jnp.concatenate([hn, pad], axis=0).astype(BF16)
    n_chunks = D_FF // FFN_COLS

    def cols(c, part):
        return slice(part * D_FF + c * FFN_COLS, part * D_FF + (c + 1) * FFN_COLS)

    def up(c, slot):
        for part in range(2):
            u_buf[slot, part] = jnp.dot(hext_ref[...], wup_ref[:, cols(c, part)],
                                        preferred_element_type=F32)

    def conv_gate(c, slot):
        outs = []
        for part in range(2):
            w = cw_ref[:, cols(c, part)]
            u = u_buf[slot, part]
            rows = u.shape[0]
            conv = (w[0:1] * pltpu.roll(u, 1, axis=0) + w[1:2] * u
                    + w[2:3] * pltpu.roll(u, rows - 1, axis=0))
            outs.append(conv[HALO:HALO + tm] + cb_ref[:, cols(c, part)])
        a, v = outs
        g_ref[:, c * FFN_COLS:(c + 1) * FFN_COLS] = (a * jax.nn.sigmoid(a) * v).astype(BF16)

    n_slots = u_buf.shape[0]
    up(0, 0)
    for c in range(n_chunks):
        if c + 1 < n_chunks:
            up(c + 1, (c + 1) % n_slots)
        conv_gate(c, c % n_slots)
    o_ref[...] = x_ref[...] + jnp.dot(g_ref[...], wdn_ref[...], preferred_element_type=F32)


def _ffn(x1, ffn_g, w_up_stack, conv_w, conv_b, w_down_stack, layer, seq):
    rows, d_model = x1.shape
    tm = PROJ_ROWS
    tiles_per_seq = seq // tm
    per_tile = tm // SUBLANES
    n_halo_blocks = rows // SUBLANES
    kernel = functools.partial(_ffn_kernel, tiles_per_seq=tiles_per_seq)
    return pl.pallas_call(
        kernel,
        grid=(rows // tm,),
        in_specs=[pl.BlockSpec((tm, d_model), lambda i: (i, 0)),
                  pl.BlockSpec((SUBLANES, d_model), lambda i: (jnp.maximum(i * per_tile - 1, 0), 0)),
                  pl.BlockSpec((SUBLANES, d_model),
                               lambda i: (jnp.minimum((i + 1) * per_tile, n_halo_blocks - 1), 0)),
                  _resident((1, d_model)),
                  _layer_resident(w_up_stack, layer), _resident(conv_w.shape), _resident(conv_b.shape),
                  _layer_resident(w_down_stack, layer)],
        out_specs=pl.BlockSpec((tm, d_model), lambda i: (i, 0)),
        out_shape=jax.ShapeDtypeStruct((rows, d_model), F32),
        scratch_shapes=[pltpu.VMEM((tm + 2 * HALO, d_model), BF16),
                        pltpu.VMEM((2, 2, tm + 2 * HALO, FFN_COLS), F32),
                        pltpu.VMEM((tm, D_FF), BF16)],
        compiler_params=_cparams("parallel"),
        name="ffn",
    )(x1, x1, x1, ffn_g, w_up_stack, conv_w, conv_b, w_down_stack)


def _rope_tables(positions):
    half = DA_HEAD_DIM // 2
    inv_freq = ROPE_THETA ** (-jnp.arange(half, dtype=F32) / half)
    ang = positions.astype(F32)[:, :, None] * inv_freq
    cos, sin = lax.optimization_barrier((jnp.cos(ang), jnp.sin(ang)))
    assert LANES == 4 * half
    return jnp.concatenate([cos, cos, -sin, sin], axis=-1).reshape(-1, LANES)


def kernel(x, positions, mix_norm_g, w_in, q_norm_g, k_norm_g, lam_q1, lam_k1, lam_q2, lam_k2,
           diff_out_g, hg_lb_logits, hg_out_g, w_out, ffn_norm_g, w_up, conv_w, conv_b, w_down):
    batch, seq, d_model = x.shape
    assert seq % PROJ_ROWS == 0 and seq % ATTN_Q_COLS == 0 and seq % HG_BLOCK == 0
    rope_t = _rope_tables(positions)
    p = jax.nn.softmax(hg_lb_logits.astype(F32), axis=0)
    lower_bounds = jnp.clip(jnp.cumsum(p, axis=0) - p[0:1], 0.0, 1.0 - 1e-4)
    gid = np.arange(DA_QK) // DA_HEAD_DIM
    gsum = jnp.asarray(gid[:, None] == gid[None, :], BF16)
    n_groups = DA_QK // DA_HEAD_DIM

    w_in16, w_out16, w_up16, w_down16 = (w.astype(BF16) for w in (w_in, w_out, w_up, w_down))

    x2d = x.reshape(batch * seq, d_model)
    for l in range(DEPTH):
        lam_init = 0.8 - 0.6 * float(np.exp(-0.3 * l))
        lam = (jnp.exp(jnp.sum(lam_q1[l].astype(F32) * lam_k1[l].astype(F32)))
               - jnp.exp(jnp.sum(lam_q2[l].astype(F32) * lam_k2[l].astype(F32))) + lam_init)
        qt, k, vt, hg = _in_proj(
            x2d, mix_norm_g[l][None, :], w_in16, l, rope_t,
            jnp.tile(q_norm_g[l], n_groups)[None, :], jnp.tile(k_norm_g[l], n_groups)[None, :],
            gsum, batch, seq)
        bounded = (_score_bound(q_norm_g[l], k_norm_g[l]) <= SCORE_BOUND).astype(jnp.int32)
        y_a = _attention(qt, k.reshape(batch, seq, DA_QK), vt, lam.reshape(1, 1), bounded.reshape(1, 1),
                         diff_out_g[l][None, :], 1.0 - lam_init)
        o_f, o_b = _hgrn(hg.reshape(batch, seq, -1), lower_bounds[l])
        x1 = _out_proj(y_a.reshape(batch * seq, DA_WIDTH), o_f.reshape(batch * seq, HG_WIDTH),
                       o_b.reshape(batch * seq, HG_WIDTH), hg, x2d, w_out16, l,
                       hg_out_g[l][None, :])
        x2d = _ffn(x1, ffn_norm_g[l][None, :], w_up16, conv_w[l], conv_b[l][None, :],
                   w_down16, l, seq)
    return x2d.reshape(batch, seq, d_model)
```
